```python
import math
import jax, jax.numpy as jnp
from jax import lax
import numpy as np

D_MODEL = 1024
BATCH = 4
SEQ = 8192
DEPTH = 4

GRID_W = 64
CTX_LEN = 256
N_MIXERS = 3
ROPE_BASE = 10000.0
EPS = 1e-6
Q_BLOCK = 128

DA_HEADS = 8
DA_HEAD_DIM = 64
MLA_HEADS = 16
MLA_Q_RANK = 256
MLA_KV_RANK = 128
MLA_NOPE = 64
MLA_ROPE = 32
MLA_V = 64
MLA_QK = MLA_NOPE + MLA_ROPE
GDN_HEADS = 8
GDN_DK = 128
GDN_DV = 128
GDN_CONV = 5
GDN_CHUNK = 64
N_EXPERTS = 16
EXPERT_FF = 2048
EC_FACTOR = 2

kernel_name = 'hybrid_diffattn_mla_gdn_ecmoe_prefix'

F32 = jnp.float32


def rms_norm(x, g):
    xf = x.astype(F32)
    y = xf * lax.rsqrt(jnp.mean(xf * xf, axis=-1, keepdims=True) + EPS)
    return (y * g.astype(F32)).astype(x.dtype)


def l2_norm(x):
    xf = x.astype(F32)
    return (xf * lax.rsqrt(jnp.sum(xf * xf, axis=-1, keepdims=True) + EPS)).astype(x.dtype)


def axial_rope_tables(rows, dim):
    nf = dim // 4
    inv = ROPE_BASE ** (-jnp.arange(nf, dtype=F32) / nf)
    r = jnp.repeat(jnp.arange(rows, dtype=F32), GRID_W)
    c = jnp.tile(jnp.arange(GRID_W, dtype=F32), rows)
    ang = jnp.concatenate([r[:, None] * inv, c[:, None] * inv], axis=-1)
    return jnp.cos(ang), jnp.sin(ang)


def apply_rope(x, cos, sin):
    half = x.shape[-1] // 2
    x1, x2 = x[..., :half], x[..., half:]
    cs, sn = cos[:, None, :], sin[:, None, :]
    return jnp.concatenate([x1 * cs - x2 * sn, x2 * cs + x1 * sn], axis=-1).astype(x.dtype)


def over_query_blocks(fn, *qs):
    b, n = qs[0].shape[:2]
    nb = n // Q_BLOCK
    blocks = tuple(q.reshape((b, nb, Q_BLOCK) + q.shape[2:]).swapaxes(0, 1) for q in qs)
    out = lax.map(lambda t: fn(*t), blocks)
    return out.swapaxes(0, 1).reshape((b, n) + out.shape[3:])


def softmax_core(q, k, v, scale):
    s = jnp.einsum('bqhd,bkhd->bhqk', q, k).astype(F32) * scale
    p = jax.nn.softmax(s, axis=-1)
    return jnp.einsum('bhqk,bkhe->bqhe', p.astype(v.dtype), v)


def diff_core(q1, q2, k1, k2, v, lam):
    scale = DA_HEAD_DIM ** -0.5
    s1 = jnp.einsum('bqhd,bkhd->bhqk', q1, k1).astype(F32) * scale
    s2 = jnp.einsum('bqhd,bkhd->bhqk', q2, k2).astype(F32) * scale
    p = jax.nn.softmax(s1, axis=-1) - lam * jax.nn.softmax(s2, axis=-1)
    return jnp.einsum('bhqk,bkhe->bqhe', p.astype(v.dtype), v)


def diff_attention(hc, hl, w_in, q_gain, k_gain, lam_vecs, sub_gain, w_out, lam_init, cos, sin, need_ctx):
    lv = lam_vecs.astype(F32)
    lam = jnp.exp(jnp.sum(lv[0] * lv[1])) - jnp.exp(jnp.sum(lv[2] * lv[3])) + lam_init

    def project(h, rope):
        b, n, _ = h.shape
        q, k, v = jnp.split(h @ w_in, 3, axis=-1)
        q = rms_norm(q.reshape(b, n, 2 * DA_HEADS, DA_HEAD_DIM), q_gain)
        k = rms_norm(k.reshape(b, n, 2 * DA_HEADS, DA_HEAD_DIM), k_gain)
        if rope:
            q = apply_rope(q, cos, sin)
            k = apply_rope(k, cos, sin)
        q = q.reshape(b, n, DA_HEADS, 2, DA_HEAD_DIM)
        k = k.reshape(b, n, DA_HEADS, 2, DA_HEAD_DIM)
        v = v.reshape(b, n, DA_HEADS, 2 * DA_HEAD_DIM)
        return q[..., 0, :], q[..., 1, :], k[..., 0, :], k[..., 1, :], v

    def finish(o):
        b, n = o.shape[:2]
        o = rms_norm(o, sub_gain) * (1.0 - lam_init)
        return o.reshape(b, n, -1) @ w_out

    q1c, q2c, k1c, k2c, vc = project(hc, False)
    q1l, q2l, k1l, k2l, vl = project(hl, True)
    k1 = jnp.concatenate([k1c, k1l], axis=1)
    k2 = jnp.concatenate([k2c, k2l], axis=1)
    v = jnp.concatenate([vc, vl], axis=1)
    out_l = finish(over_query_blocks(lambda a, b_: diff_core(a, b_, k1, k2, v, lam), q1l, q2l))
    out_c = finish(diff_core(q1c, q2c, k1c, k2c, vc, lam)) if need_ctx else None
    return out_c, out_l


def mla_attention(hc, hl, w_down, q_a_gain, kv_a_gain, w_uq, w_ukv, q_gain, k_gain, w_out, cos, sin, need_ctx):
    def project(h, rope):
        b, n, _ = h.shape
        lat = h @ w_down
        cq = lat[..., :MLA_Q_RANK]
        ckv = lat[..., MLA_Q_RANK:MLA_Q_RANK + MLA_KV_RANK]
        kr = lat[..., MLA_Q_RANK + MLA_KV_RANK:]
        q = (rms_norm(cq, q_a_gain) @ w_uq).reshape(b, n, MLA_HEADS, MLA_QK)
        kv = (rms_norm(ckv, kv_a_gain) @ w_ukv).reshape(b, n, MLA_HEADS, MLA_NOPE + MLA_V)
        k = jnp.concatenate([kv[..., :MLA_NOPE], jnp.broadcast_to(kr[:, :, None, :], (b, n, MLA_HEADS, MLA_ROPE))], axis=-1)
        v = kv[..., MLA_NOPE:]
        q = rms_norm(q, q_gain)
        k = rms_norm(k, k_gain)
        if rope:
            q = jnp.concatenate([q[..., :MLA_NOPE], apply_rope(q[..., MLA_NOPE:], cos, sin)], axis=-1)
            k = jnp.concatenate([k[..., :MLA_NOPE], apply_rope(k[..., MLA_NOPE:], cos, sin)], axis=-1)
        return q, k, v

    def finish(o):
        b, n = o.shape[:2]
        return o.reshape(b, n, -1) @ w_out

    scale = MLA_QK ** -0.5
    qc, kc, vc = project(hc, False)
    ql, kl, vl = project(hl, True)
    k = jnp.concatenate([kc, kl], axis=1)
    v = jnp.concatenate([vc, vl], axis=1)
    out_l = finish(over_query_blocks(lambda a: softmax_core(a, k, v, scale), ql))
    out_c = finish(softmax_core(qc, kc, vc, scale)) if need_ctx else None
    return out_c, out_l


def centred_depthwise_conv(x, w):
    k = w.shape[0]
    return lax.conv_general_dilated(x, w[:, None, :].astype(x.dtype), window_strides=(1,),
                                    padding=[((k - 1) // 2, k // 2)],
                                    dimension_numbers=('NWC', 'WIO', 'NWC'),
                                    feature_group_count=x.shape[-1])


def gated_delta_chunked(q, k, v, g, beta, s0):
    b, n, h = q.shape[:3]
    c = GDN_CHUNK

    def to_chunks(t):
        t = t.astype(F32).reshape((b, n // c, c, h) + t.shape[3:])
        return t.transpose((1, 0, 3, 2) + tuple(range(4, t.ndim)))

    q, k, v, g, beta = to_chunks(q), to_chunks(k), to_chunks(v), to_chunks(g), to_chunks(beta)
    decay = jnp.cumsum(g, axis=-1)
    tril = jnp.tril(jnp.ones((c, c), bool))
    strict = jnp.tril(jnp.ones((c, c), bool), -1)
    diff = decay[..., :, None] - decay[..., None, :]
    gamma = jnp.where(tril, jnp.exp(jnp.where(tril, diff, 0.0)), 0.0)
    kb = k * beta[..., None]
    vb = v * beta[..., None]
    a_mat = jnp.where(strict, jnp.einsum('...id,...jd->...ij', kb, k) * gamma, 0.0)
    eye = jnp.eye(c, dtype=F32)
    rhs = jnp.concatenate([vb, kb * jnp.exp(decay)[..., None]], axis=-1)
    sol = lax.linalg.triangular_solve(a_mat + eye, rhs, left_side=True, lower=True, unit_diagonal=True)
    dv = v.shape[-1]
    u, w = sol[..., :dv], sol[..., dv:]
    qk = jnp.where(tril, jnp.einsum('...id,...jd->...ij', q, k) * gamma, 0.0)
    q_dec = q * jnp.exp(decay)[..., None]
    k_dec = k * jnp.exp(decay[..., -1:] - decay)[..., None]
    last = jnp.exp(decay[..., -1])

    def step(s, inp):
        qd, kd, ww, uu, qkc, lst = inp
        v_new = uu - jnp.einsum('bhck,bhkv->bhcv', ww, s)
        o = jnp.einsum('bhck,bhkv->bhcv', qd, s) + jnp.einsum('bhcj,bhjv->bhcv', qkc, v_new)
        s = s * lst[..., None, None] + jnp.einsum('bhck,bhcv->bhkv', kd, v_new)
        return s, o

    s_fin, o = lax.scan(step, s0.astype(F32), (q_dec, k_dec, w, u, qk, last))
    o = o.transpose(1, 0, 3, 2, 4).reshape(b, n, h, dv)
    return o, s_fin


def gated_deltanet(hc, hl, w_in, conv_w, a_log, dt_bias, o_gain, w_out, need_ctx):
    nqk = GDN_HEADS * GDN_DK
    nv = GDN_HEADS * GDN_DV
    nqkv = 2 * nqk + nv

    def prep(h):
        b, n, _ = h.shape
        p = h @ w_in
        qkv = jax.nn.silu(centred_depthwise_conv(p[..., :nqkv], conv_w))
        q = l2_norm(qkv[..., :nqk].reshape(b, n, GDN_HEADS, GDN_DK)) * (GDN_DK ** -0.5)
        k = l2_norm(qkv[..., nqk:2 * nqk].reshape(b, n, GDN_HEADS, GDN_DK))
        v = qkv[..., 2 * nqk:].reshape(b, n, GDN_HEADS, GDN_DV)
        z = p[..., nqkv:nqkv + nv].reshape(b, n, GDN_HEADS, GDN_DV)
        ab = p[..., nqkv + nv:].astype(F32).reshape(b, n, 2, 2, GDN_HEADS)
        g = -jnp.exp(a_log.astype(F32)) * jax.nn.softplus(ab[:, :, 0] + dt_bias.astype(F32))
        beta = jax.nn.sigmoid(ab[:, :, 1])
        return (q, k, v, g, beta), z

    def scan_dir(t, d, s0):
        q, k, v, g, beta = t
        g, beta = g[:, :, d], beta[:, :, d]
        if d == 1:
            q, k, v, g, beta = [jnp.flip(a, axis=1) for a in (q, k, v, g, beta)]
        o, s = gated_delta_chunked(q, k, v, g, beta, s0)
        if d == 1:
            o = jnp.flip(o, axis=1)
        return o, s

    tc, zc = prep(hc)
    tl, zl = prep(hl)
    s0 = jnp.zeros((hl.shape[0], GDN_HEADS, GDN_DK, GDN_DV), F32)
    oc_f, sc_f = scan_dir(tc, 0, s0)
    ol_f, _ = scan_dir(tl, 0, sc_f)
    oc_b, sc_b = scan_dir(tc, 1, s0)
    ol_b, _ = scan_dir(tl, 1, sc_b)

    def finish(o, z):
        b, n = o.shape[:2]
        o = rms_norm(o.astype(z.dtype), o_gain) * jax.nn.silu(z)
        return o.reshape(b, n, -1) @ w_out

    out_l = finish(ol_f + ol_b, zl)
    out_c = finish(oc_f + oc_b, zc) if need_ctx else None
    return out_c, out_l


def expert_choice_ffn(h, w_router, w_gate, w_up, w_down):
    b, n, _ = h.shape
    cap = EC_FACTOR * n // N_EXPERTS
    aff = jax.nn.softmax((h @ w_router).astype(F32), axis=-1)
    gate, idx = lax.top_k(aff.transpose(0, 2, 1), cap)
    bi = jnp.arange(b)[:, None, None]
    xs = h[bi, idx]
    hg = jnp.einsum('becd,edf->becf', xs, w_gate)
    hu = jnp.einsum('becd,edf->becf', xs, w_up)
    y = jnp.einsum('becf,efd->becd', jax.nn.silu(hg) * hu, w_down) * gate[..., None].astype(h.dtype)
    return jnp.zeros_like(h).at[bi, idx].add(y)


def _count(m):
    return len(range(m, DEPTH, N_MIXERS))


def setup_inputs(seed: int = 0) -> dict:
    key = jax.random.key(seed)
    ks = iter(jax.random.split(key, 48))
    D = D_MODEL
    nA, nB, nC = _count(0), _count(1), _count(2)

    def nrm(shape, s):
        return jax.random.normal(next(ks), shape, F32) * s

    def gain(shape):
        return 1.0 + nrm(shape, 0.02)

    gdn_in = 2 * GDN_HEADS * GDN_DK + 2 * GDN_HEADS * GDN_DV + 4 * GDN_HEADS
    gdn_qkv = 2 * GDN_HEADS * GDN_DK + GDN_HEADS * GDN_DV
    dt = jnp.exp(jax.random.uniform(next(ks), (nC, 2, GDN_HEADS), F32, math.log(1e-3), math.log(1e-1)))
    return {
        'x': nrm((BATCH, SEQ, D), 1.0),
        'c': nrm((BATCH, D), 1.0),
        'ctx': nrm((BATCH, CTX_LEN, D), 1.0),
        'c_ctx': nrm((D,), 1.0),
        'ada_w': nrm((DEPTH, D, 6 * D), 0.5 * D ** -0.5),
        'ada_b': nrm((DEPTH, 6 * D), 0.02),
        'norm_g': gain((DEPTH, 2, D)),
        'da_w_in': nrm((nA, D, 3 * D), D ** -0.5),
        'da_q_gain': gain((nA, DA_HEAD_DIM)),
        'da_k_gain': gain((nA, DA_HEAD_DIM)),
        'da_lambda': nrm((nA, 4, DA_HEAD_DIM), 0.1),
        'da_sub_gain': gain((nA, 2 * DA_HEAD_DIM)),
        'da_w_out': nrm((nA, D, D), D ** -0.5),
        'mla_w_down': nrm((nB, D, MLA_Q_RANK + MLA_KV_RANK + MLA_ROPE), D ** -0.5),
        'mla_q_a_gain': gain((nB, MLA_Q_RANK)),
        'mla_kv_a_gain': gain((nB, MLA_KV_RANK)),
        'mla_w_uq': nrm((nB, MLA_Q_RANK, MLA_HEADS * MLA_QK), MLA_Q_RANK ** -0.5),
        'mla_w_ukv': nrm((nB, MLA_KV_RANK, MLA_HEADS * (MLA_NOPE + MLA_V)), MLA_KV_RANK ** -0.5),
        'mla_q_gain': gain((nB, MLA_QK)),
        'mla_k_gain': gain((nB, MLA_QK)),
        'mla_w_out': nrm((nB, MLA_HEADS * MLA_V, D), (MLA_HEADS * MLA_V) ** -0.5),
        'gdn_w_in': nrm((nC, D, gdn_in), D ** -0.5),
        'gdn_conv_w': nrm((nC, GDN_CONV, gdn_qkv), GDN_CONV ** -0.5),
        'gdn_a_log': jnp.log(jax.random.uniform(next(ks), (nC, 2, GDN_HEADS), F32, 1.0, 16.0)),
        'gdn_dt_bias': dt + jnp.log(-jnp.expm1(-dt)),
        'gdn_o_gain': gain((nC, GDN_DV)),
        'gdn_w_out': nrm((nC, GDN_HEADS * GDN_DV, D), (GDN_HEADS * GDN_DV) ** -0.5),
        'moe_router': nrm((DEPTH, D, N_EXPERTS), D ** -0.5),
        'moe_w_gate': nrm((DEPTH, N_EXPERTS, D, EXPERT_FF), D ** -0.5),
        'moe_w_up': nrm((DEPTH, N_EXPERTS, D, EXPERT_FF), D ** -0.5),
        'moe_w_down': nrm((DEPTH, N_EXPERTS, EXPERT_FF, D), EXPERT_FF ** -0.5),
    }


def reference(x, c, ctx, c_ctx, ada_w, ada_b, norm_g,
              da_w_in, da_q_gain, da_k_gain, da_lambda, da_sub_gain, da_w_out,
              mla_w_down, mla_q_a_gain, mla_kv_a_gain, mla_w_uq, mla_w_ukv, mla_q_gain, mla_k_gain, mla_w_out,
              gdn_w_in, gdn_conv_w, gdn_a_log, gdn_dt_bias, gdn_o_gain, gdn_w_out,
              moe_router, moe_w_gate, moe_w_up, moe_w_down):
    rows = x.shape[1] // GRID_W
    da_cos, da_sin = axial_rope_tables(rows, DA_HEAD_DIM)
    mla_cos, mla_sin = axial_rope_tables(rows, MLA_ROPE)
    cs = ctx
    sc = jax.nn.silu(c)
    scc = jax.nn.silu(c_ctx)
    for l in range(DEPTH):
        last = l == DEPTH - 1
        j = l // N_MIXERS
        kind = l % N_MIXERS
        mod_l = (sc @ ada_w[l] + ada_b[l])[:, None, :]
        mod_c = scc @ ada_w[l] + ada_b[l]
        sh1l, sc1l, g1l, sh2l, sc2l, g2l = jnp.split(mod_l, 6, axis=-1)
        sh1c, sc1c, g1c, sh2c, sc2c, g2c = jnp.split(mod_c, 6, axis=-1)
        hl = rms_norm(x, norm_g[l, 0]) * (1.0 + sc1l) + sh1l
        hc = rms_norm(cs, norm_g[l, 0]) * (1.0 + sc1c) + sh1c
        if kind == 0:
            lam_init = 0.8 - 0.6 * math.exp(-0.3 * l)
            mc, ml = diff_attention(hc, hl, da_w_in[j], da_q_gain[j], da_k_gain[j], da_lambda[j],
                                    da_sub_gain[j], da_w_out[j], lam_init, da_cos, da_sin, not last)
        elif kind == 1:
            mc, ml = mla_attention(hc, hl, mla_w_down[j], mla_q_a_gain[j], mla_kv_a_gain[j], mla_w_uq[j],
                                   mla_w_ukv[j], mla_q_gain[j], mla_k_gain[j], mla_w_out[j],
                                   mla_cos, mla_sin, not last)
        else:
            mc, ml = gated_deltanet(hc, hl, gdn_w_in[j], gdn_conv_w[j], gdn_a_log[j], gdn_dt_bias[j],
                                    gdn_o_gain[j], gdn_w_out[j], not last)
        x = x + g1l * ml
        hl2 = rms_norm(x, norm_g[l, 1]) * (1.0 + sc2l) + sh2l
        x = x + g2l * expert_choice_ffn(hl2, moe_router[l], moe_w_gate[l], moe_w_up[l], moe_w_down[l])
        if not last:
            cs = cs + g1c * mc
            hc2 = rms_norm(cs, norm_g[l, 1]) * (1.0 + sc2c) + sh2c
            cs = cs + g2c * expert_choice_ffn(hc2, moe_router[l], moe_w_gate[l], moe_w_up[l], moe_w_down[l])
    return x
```

```python
import functools
import math

import jax
import jax.numpy as jnp
from jax import lax
from jax.experimental import pallas as pl
from jax.experimental.pallas import tpu as pltpu

F32 = jnp.float32
BF16 = jnp.bfloat16

GRID_W = 64
N_MIXERS = 3
ROPE_BASE = 10000.0
EPS = 1e-6

DA_HEADS = 8
DA_HEAD_DIM = 64
MLA_HEADS = 16
MLA_Q_RANK = 256
MLA_KV_RANK = 128
MLA_NOPE = 64
MLA_ROPE = 32
MLA_V = 64
MLA_QK = MLA_NOPE + MLA_ROPE
GDN_HEADS = 8
GDN_DK = 128
GDN_DV = 128
GDN_CHUNK = 64
N_EXPERTS = 16
EC_FACTOR = 2

LANES = 128
ATTN_TQ = 256
ATTN_TK = 256
LOG2E = math.log2(math.e)
NEG_INIT = -1e30
VMEM_LIMIT = 56 * 1024 * 1024


def _online_softmax_step(s, m, l):
    m_new = jnp.maximum(m, jnp.max(s, axis=-1, keepdims=True))
    alpha = jnp.exp2(m - m_new)
    p = jnp.exp2(s - m_new)
    l_new = alpha * l + jnp.sum(p, axis=-1, keepdims=True)
    return p, alpha, m_new, l_new


def _k_range(n_lat_tiles, nk):
    is_ctx = pl.program_id(2) == n_lat_tiles
    return jnp.where(is_ctx, nk - 1, 0)


def _nt_dot(a, b):
    return lax.dot_general(a, b, (((1,), (1,)), ((), ())), preferred_element_type=F32)


def _da_attn_kernel(lam_ref, gain_ref, q_ref, k_ref, v_ref, o_ref, *, n_lat_tiles, nk, tk, out_scale):
    tq = q_ref.shape[1]
    q = q_ref[0]
    lane = lax.broadcasted_iota(jnp.int32, (1, LANES), 1)
    zero = jnp.zeros_like(q)
    q1 = jnp.where(lane < DA_HEAD_DIM, q, zero)
    q2 = jnp.where(lane >= DA_HEAD_DIM, q, zero)

    def body(j, carry):
        m1, l1, a1, m2, l2, a2 = carry
        off = pl.multiple_of(j * tk, tk)
        kk = k_ref[0, pl.ds(off, tk), :]
        vv = v_ref[0, pl.ds(off, tk), :]
        p1, al1, m1, l1 = _online_softmax_step(_nt_dot(q1, kk), m1, l1)
        p2, al2, m2, l2 = _online_softmax_step(_nt_dot(q2, kk), m2, l2)
        a1 = al1 * a1 + jnp.dot(p1.astype(BF16), vv, preferred_element_type=F32)
        a2 = al2 * a2 + jnp.dot(p2.astype(BF16), vv, preferred_element_type=F32)
        return m1, l1, a1, m2, l2, a2

    m0 = jnp.full((tq, 1), NEG_INIT, F32)
    l0 = jnp.zeros((tq, 1), F32)
    a0 = jnp.zeros((tq, LANES), F32)
    _, l1, a1, _, l2, a2 = lax.fori_loop(_k_range(n_lat_tiles, nk), nk, body, (m0, l0, a0, m0, l0, a0))
    o = a1 / l1 - lam_ref[0] * (a2 / l2)
    y = o * lax.rsqrt(jnp.mean(o * o, axis=-1, keepdims=True) + EPS)
    o_ref[0] = (y * gain_ref[...] * out_scale).astype(o_ref.dtype)


def _mla_attn_kernel(q_ref, k_ref, v_ref, o_ref, *, n_lat_tiles, nk, tk):
    tq = q_ref.shape[1]
    qa = q_ref[0, :, :LANES]
    qb = q_ref[0, :, LANES:]
    lane = lax.broadcasted_iota(jnp.int32, (1, LANES), 1)
    lo = lane < MLA_V

    def body(j, carry):
        m1, l1, m2, l2, acc = carry
        off = pl.multiple_of(j * tk, tk)
        ka = k_ref[0, pl.ds(off, tk), :LANES]
        kb = k_ref[0, pl.ds(off, tk), LANES:]
        vv = v_ref[0, pl.ds(off, tk), :]
        zero = jnp.zeros_like(vv)
        p1, al1, m1, l1 = _online_softmax_step(_nt_dot(qa, ka), m1, l1)
        p2, al2, m2, l2 = _online_softmax_step(_nt_dot(qb, kb), m2, l2)
        acc = jnp.where(lo, al1, al2) * acc
        acc = acc + jnp.dot(p1.astype(BF16), jnp.where(lo, vv, zero), preferred_element_type=F32)
        acc = acc + jnp.dot(p2.astype(BF16), jnp.where(lo, zero, vv), preferred_element_type=F32)
        return m1, l1, m2, l2, acc

    m0 = jnp.full((tq, 1), NEG_INIT, F32)
    l0 = jnp.zeros((tq, 1), F32)
    a0 = jnp.zeros((tq, LANES), F32)
    _, l1, _, l2, acc = lax.fori_loop(_k_range(n_lat_tiles, nk), nk, body, (m0, l0, m0, l0, a0))
    o_ref[0] = (acc / jnp.where(lo, l1, l2)).astype(o_ref.dtype)


def _attn_call(kernel, q, k, v, extra, *, q_lanes, n_groups):
    b, t, _ = q.shape
    tq, tk = ATTN_TQ, ATTN_TK
    n_tiles = t // tq
    nk = t // tk
    body = functools.partial(kernel, n_lat_tiles=n_tiles - 1, nk=nk, tk=tk)
    extra_specs = [pl.BlockSpec(memory_space=pltpu.SMEM) if e.ndim == 1 else
                   pl.BlockSpec(e.shape, lambda bi, g, qi: (0, 0)) for e in extra]
    return pl.pallas_call(
        body,
        grid=(b, n_groups, n_tiles),
        in_specs=extra_specs + [
            pl.BlockSpec((1, tq, q_lanes), lambda bi, g, qi: (bi, qi, g)),
            pl.BlockSpec((1, t, q_lanes), lambda bi, g, qi: (bi, 0, g)),
            pl.BlockSpec((1, t, LANES), lambda bi, g, qi: (bi, 0, g)),
        ],
        out_specs=pl.BlockSpec((1, tq, LANES), lambda bi, g, qi: (bi, qi, g)),
        out_shape=jax.ShapeDtypeStruct((b, t, n_groups * LANES), BF16),
        compiler_params=pltpu.CompilerParams(
            dimension_semantics=("parallel", "parallel", "arbitrary"), vmem_limit_bytes=VMEM_LIMIT),
    )(*extra, q, k, v)


def _rms_norm(x, g):
    xf = x.astype(F32)
    y = xf * lax.rsqrt(jnp.mean(xf * xf, axis=-1, keepdims=True) + EPS)
    return y * g.astype(F32)


def _l2_norm(x):
    return x * lax.rsqrt(jnp.sum(x * x, axis=-1, keepdims=True) + EPS)


def _rope_tables(rows, dim):
    nf = dim // 4
    inv = ROPE_BASE ** (-jnp.arange(nf, dtype=F32) / nf)
    r = jnp.repeat(jnp.arange(rows, dtype=F32), GRID_W)
    c = jnp.tile(jnp.arange(GRID_W, dtype=F32), rows)
    ang = jnp.concatenate([r[:, None] * inv, c[:, None] * inv], axis=-1)
    return jnp.cos(ang), jnp.sin(ang)


def _apply_rope(x, cos, sin):
    half = x.shape[-1] // 2
    x1, x2 = x[..., :half], x[..., half:]
    cs, sn = cos[:, None, :], sin[:, None, :]
    return jnp.concatenate([x1 * cs - x2 * sn, x2 * cs + x1 * sn], axis=-1)


def _diff_attention(hc, hl, w_in, q_gain, k_gain, lam_vecs, sub_gain, w_out, lam_init, cos, sin):
    lv = lam_vecs.astype(F32)
    lam = jnp.exp(jnp.sum(lv[0] * lv[1])) - jnp.exp(jnp.sum(lv[2] * lv[3])) + lam_init
    scale = DA_HEAD_DIM ** -0.5 * LOG2E

    def project(h, rope):
        b, n, _ = h.shape
        q, k, v = jnp.split(h @ w_in, 3, axis=-1)
        q = _rms_norm(q.reshape(b, n, 2 * DA_HEADS, DA_HEAD_DIM), q_gain)
        k = _rms_norm(k.reshape(b, n, 2 * DA_HEADS, DA_HEAD_DIM), k_gain)
        if rope:
            q = _apply_rope(q, cos, sin)
            k = _apply_rope(k, cos, sin)
        return (q * scale).reshape(b, n, -1), k.reshape(b, n, -1), v

    ql, kl, vl = project(hl, True)
    qc, kc, vc = project(hc, False)
    q = jnp.concatenate([ql, qc], axis=1).astype(BF16)
    k = jnp.concatenate([kl, kc], axis=1).astype(BF16)
    v = jnp.concatenate([vl, vc], axis=1).astype(BF16)
    kern = functools.partial(_da_attn_kernel, out_scale=1.0 - lam_init)
    o = _attn_call(kern, q, k, v, [lam.reshape(1), sub_gain.reshape(1, -1).astype(F32)],
                   q_lanes=LANES, n_groups=DA_HEADS)
    out = o.astype(F32) @ w_out
    n = hl.shape[1]
    return out[:, n:], out[:, :n]


def _mla_attention(hc, hl, w_down, q_a_gain, kv_a_gain, w_uq, w_ukv, q_gain, k_gain, w_out, cos, sin):
    scale = MLA_QK ** -0.5 * LOG2E
    pad = LANES - MLA_QK

    def project(h, rope):
        b, n, _ = h.shape
        lat = h @ w_down
        cq = lat[..., :MLA_Q_RANK]
        ckv = lat[..., MLA_Q_RANK:MLA_Q_RANK + MLA_KV_RANK]
        kr = lat[..., MLA_Q_RANK + MLA_KV_RANK:]
        q = (_rms_norm(cq, q_a_gain) @ w_uq).reshape(b, n, MLA_HEADS, MLA_QK)
        kv = (_rms_norm(ckv, kv_a_gain) @ w_ukv).reshape(b, n, MLA_HEADS, MLA_NOPE + MLA_V)
        k = jnp.concatenate([kv[..., :MLA_NOPE],
                             jnp.broadcast_to(kr[:, :, None, :], (b, n, MLA_HEADS, MLA_ROPE))], axis=-1)
        v = kv[..., MLA_NOPE:]
        q = _rms_norm(q, q_gain)
        k = _rms_norm(k, k_gain)
        if rope:
            q = jnp.concatenate([q[..., :MLA_NOPE], _apply_rope(q[..., MLA_NOPE:], cos, sin)], axis=-1)
            k = jnp.concatenate([k[..., :MLA_NOPE], _apply_rope(k[..., MLA_NOPE:], cos, sin)], axis=-1)
        q = jnp.pad(q * scale, ((0, 0), (0, 0), (0, 0), (0, pad)))
        k = jnp.pad(k, ((0, 0), (0, 0), (0, 0), (0, pad)))
        return q.reshape(b, n, -1), k.reshape(b, n, -1), v.reshape(b, n, -1)

    ql, kl, vl = project(hl, True)
    qc, kc, vc = project(hc, False)
    q = jnp.concatenate([ql, qc], axis=1).astype(BF16)
    k = jnp.concatenate([kl, kc], axis=1).astype(BF16)
    v = jnp.concatenate([vl, vc], axis=1).astype(BF16)
    o = _attn_call(_mla_attn_kernel, q, k, v, [], q_lanes=2 * LANES, n_groups=MLA_HEADS // 2)
    out = o.astype(F32) @ w_out
    n = hl.shape[1]
    return out[:, n:], out[:, :n]


def _centred_depthwise_conv(x, w):
    k = w.shape[0]
    return lax.conv_general_dilated(x, w[:, None, :].astype(x.dtype), window_strides=(1,),
                                    padding=[((k - 1) // 2, k // 2)],
                                    dimension_numbers=('NWC', 'WIO', 'NWC'),
                                    feature_group_count=x.shape[-1])


def _gated_delta_chunked(q, k, v, g, beta, s0):
    b, n, h = q.shape[:3]
    c = GDN_CHUNK

    def to_chunks(t):
        t = t.astype(F32).reshape((b, n // c, c, h) + t.shape[3:])
        return t.transpose((1, 0, 3, 2) + tuple(range(4, t.ndim)))

    q, k, v, g, beta = to_chunks(q), to_chunks(k), to_chunks(v), to_chunks(g), to_chunks(beta)
    decay = jnp.cumsum(g, axis=-1)
    tril = jnp.tril(jnp.ones((c, c), bool))
    strict = jnp.tril(jnp.ones((c, c), bool), -1)
    diff = decay[..., :, None] - decay[..., None, :]
    gamma = jnp.where(tril, jnp.exp(jnp.where(tril, diff, 0.0)), 0.0)
    kb = k * beta[..., None]
    vb = v * beta[..., None]
    a_mat = jnp.where(strict, jnp.einsum('...id,...jd->...ij', kb, k) * gamma, 0.0)
    eye = jnp.eye(c, dtype=F32)
    rhs = jnp.concatenate([vb, kb * jnp.exp(decay)[..., None]], axis=-1)
    sol = lax.linalg.triangular_solve(a_mat + eye, rhs, left_side=True, lower=True, unit_diagonal=True)
    dv = v.shape[-1]
    u, w = sol[..., :dv], sol[..., dv:]
    qk = jnp.where(tril, jnp.einsum('...id,...jd->...ij', q, k) * gamma, 0.0)
    q_dec = q * jnp.exp(decay)[..., None]
    k_dec = k * jnp.exp(decay[..., -1:] - decay)[..., None]
    last = jnp.exp(decay[..., -1])

    def step(s, inp):
        qd, kd, ww, uu, qkc, lst = inp
        v_new = uu - jnp.einsum('bhck,bhkv->bhcv', ww, s)
        o = jnp.einsum('bhck,bhkv->bhcv', qd, s) + jnp.einsum('bhcj,bhjv->bhcv', qkc, v_new)
        s = s * lst[..., None, None] + jnp.einsum('bhck,bhcv->bhkv', kd, v_new)
        return s, o

    s_fin, o = lax.scan(step, s0.astype(F32), (q_dec, k_dec, w, u, qk, last))
    o = o.transpose(1, 0, 3, 2, 4).reshape(b, n, h, dv)
    return o, s_fin


def _gated_deltanet(hc, hl, w_in, conv_w, a_log, dt_bias, o_gain, w_out, need_ctx):
    nqk = GDN_HEADS * GDN_DK
    nv = GDN_HEADS * GDN_DV
    nqkv = 2 * nqk + nv

    def prep(h):
        b, n, _ = h.shape
        p = h @ w_in
        qkv = jax.nn.silu(_centred_depthwise_conv(p[..., :nqkv], conv_w))
        q = _l2_norm(qkv[..., :nqk].reshape(b, n, GDN_HEADS, GDN_DK)) * (GDN_DK ** -0.5)
        k = _l2_norm(qkv[..., nqk:2 * nqk].reshape(b, n, GDN_HEADS, GDN_DK))
        v = qkv[..., 2 * nqk:].reshape(b, n, GDN_HEADS, GDN_DV)
        z = p[..., nqkv:nqkv + nv].reshape(b, n, GDN_HEADS, GDN_DV)
        ab = p[..., nqkv + nv:].astype(F32).reshape(b, n, 2, 2, GDN_HEADS)
        g = -jnp.exp(a_log.astype(F32)) * jax.nn.softplus(ab[:, :, 0] + dt_bias.astype(F32))
        beta = jax.nn.sigmoid(ab[:, :, 1])
        return (q, k, v, g, beta), z

    def scan_dir(t, d, s0):
        q, k, v, g, beta = t
        g, beta = g[:, :, d], beta[:, :, d]
        if d == 1:
            q, k, v, g, beta = [jnp.flip(a, axis=1) for a in (q, k, v, g, beta)]
        o, s = _gated_delta_chunked(q, k, v, g, beta, s0)
        if d == 1:
            o = jnp.flip(o, axis=1)
        return o, s

    tc, zc = prep(hc)
    tl, zl = prep(hl)
    s0 = jnp.zeros((hl.shape[0], GDN_HEADS, GDN_DK, GDN_DV), F32)
    oc_f, sc_f = scan_dir(tc, 0, s0)
    ol_f, _ = scan_dir(tl, 0, sc_f)
    oc_b, sc_b = scan_dir(tc, 1, s0)
    ol_b, _ = scan_dir(tl, 1, sc_b)

    def finish(o, z):
        b, n = o.shape[:2]
        o = _rms_norm(o, o_gain) * jax.nn.silu(z)
        return o.reshape(b, n, -1) @ w_out

    out_l = finish(ol_f + ol_b, zl)
    out_c = finish(oc_f + oc_b, zc) if need_ctx else None
    return out_c, out_l


def _expert_choice_ffn(h, w_router, w_gate, w_up, w_down):
    b, n, _ = h.shape
    cap = EC_FACTOR * n // N_EXPERTS
    aff = jax.nn.softmax((h @ w_router).astype(F32), axis=-1)
    gate, idx = lax.top_k(aff.transpose(0, 2, 1), cap)
    bi = jnp.arange(b)[:, None, None]
    xs = h[bi, idx]
    hg = jnp.einsum('becd,edf->becf', xs, w_gate)
    hu = jnp.einsum('becd,edf->becf', xs, w_up)
    y = jnp.einsum('becf,efd->becd', jax.nn.silu(hg) * hu, w_down) * gate[..., None].astype(h.dtype)
    return jnp.zeros_like(h).at[bi, idx].add(y)


def kernel(x, c, ctx, c_ctx, ada_w, ada_b, norm_g, da_w_in, da_q_gain, da_k_gain, da_lambda, da_sub_gain, da_w_out, mla_w_down, mla_q_a_gain, mla_kv_a_gain, mla_w_uq, mla_w_ukv, mla_q_gain, mla_k_gain, mla_w_out, gdn_w_in, gdn_conv_w, gdn_a_log, gdn_dt_bias, gdn_o_gain, gdn_w_out, moe_router, moe_w_gate, moe_w_up, moe_w_down):
    depth = ada_w.shape[0]
    rows = x.shape[1] // GRID_W
    da_cos, da_sin = _rope_tables(rows, DA_HEAD_DIM)
    mla_cos, mla_sin = _rope_tables(rows, MLA_ROPE)
    cs = ctx
    sc = jax.nn.silu(c)
    scc = jax.nn.silu(c_ctx)
    for l in range(depth):
        last = l == depth - 1
        j = l // N_MIXERS
        kind = l % N_MIXERS
        mod_l = (sc @ ada_w[l] + ada_b[l])[:, None, :]
        mod_c = scc @ ada_w[l] + ada_b[l]
        sh1l, sc1l, g1l, sh2l, sc2l, g2l = jnp.split(mod_l, 6, axis=-1)
        sh1c, sc1c, g1c, sh2c, sc2c, g2c = jnp.split(mod_c, 6, axis=-1)
        hl = _rms_norm(x, norm_g[l, 0]) * (1.0 + sc1l) + sh1l
        hc = _rms_norm(cs, norm_g[l, 0]) * (1.0 + sc1c) + sh1c
        if kind == 0:
            lam_init = 0.8 - 0.6 * math.exp(-0.3 * l)
            mc, ml = _diff_attention(hc, hl, da_w_in[j], da_q_gain[j], da_k_gain[j], da_lambda[j],
                                     da_sub_gain[j], da_w_out[j], lam_init, da_cos, da_sin)
        elif kind == 1:
            mc, ml = _mla_attention(hc, hl, mla_w_down[j], mla_q_a_gain[j], mla_kv_a_gain[j], mla_w_uq[j],
                                    mla_w_ukv[j], mla_q_gain[j], mla_k_gain[j], mla_w_out[j], mla_cos, mla_sin)
        else:
            mc, ml = _gated_deltanet(hc, hl, gdn_w_in[j], gdn_conv_w[j], gdn_a_log[j], gdn_dt_bias[j],
                                     gdn_o_gain[j], gdn_w_out[j], not last)
        x = x + g1l * ml
        hl2 = _rms_norm(x, norm_g[l, 1]) * (1.0 + sc2l) + sh2l
        x = x + g2l * _expert_choice_ffn(hl2, moe_router[l], moe_w_gate[l], moe_w_up[l], moe_w_down[l])
        if not last:
            cs = cs + g1c * mc
            hc2 = _rms_norm(cs, norm_g[l, 1]) * (1.0 + sc2c) + sh2c
            cs = cs + g2c * _expert_choice_ffn(hc2, moe_router[l], moe_w_gate[l], moe_w_up[l], moe_w_down[l])
    return x
```

```python
import functools
import math

import jax
import jax.numpy as jnp
from jax import lax
from jax.experimental import pallas as pl
from jax.experimental.pallas import tpu as pltpu

F32 = jnp.float32
BF16 = jnp.bfloat16

GRID_W = 64
N_MIXERS = 3
ROPE_BASE = 10000.0
EPS = 1e-6

DA_HEADS = 8
DA_HEAD_DIM = 64
MLA_HEADS = 16
MLA_Q_RANK = 256
MLA_KV_RANK = 128
MLA_NOPE = 64
MLA_ROPE = 32
MLA_V = 64
MLA_QK = MLA_NOPE + MLA_ROPE
GDN_HEADS = 8
GDN_DK = 128
GDN_DV = 128
GDN_CHUNK = 64
N_EXPERTS = 16
EC_FACTOR = 2

LANES = 128
ATTN_TQ = 512
LOG2E = math.log2(math.e)
NEG_INIT = -1e30
VMEM_LIMIT = 56 * 1024 * 1024


ONES_ROWS = 8


def _attn_t_pipeline(q_maps, k_lanes, v_rows, k_ref, vt_ref, acc_refs, sa_ref, sb_ref, *, nk, tk, c):
    tq = q_maps[0].shape[1]
    for acc in acc_refs:
        acc[...] = jnp.zeros_like(acc)

    def scores(j, s_out):
        off = pl.multiple_of(j * tk, tk)
        for mp in range(2):
            kk = k_ref[0, pl.ds(off, tk), k_lanes[mp]]
            s_out[mp] = jnp.dot(kk, q_maps[mp], preferred_element_type=F32)

    def process(j, s_in, ms):
        off = pl.multiple_of(j * tk, tk)
        out = []
        for mp in range(2):
            s = s_in[mp]
            m_new = jnp.maximum(ms[mp], jnp.max(s, axis=0, keepdims=True))
            alpha = jnp.exp2((ms[mp] - m_new) * c)
            p = jnp.exp2((s - m_new) * c).astype(BF16)
            vv = vt_ref[0, 0, v_rows[mp], pl.ds(off, tk)]
            acc_refs[mp][...] = alpha * acc_refs[mp][...] + jnp.dot(vv, p, preferred_element_type=F32)
            out.append(m_new)
        return tuple(out)

    def pair(i, ms):
        j = 2 * i
        scores(j + 1, sb_ref)
        ms = process(j, sa_ref, ms)
        scores(j + 2, sa_ref)
        return process(j + 1, sb_ref, ms)

    m0 = jnp.full((1, tq), NEG_INIT, F32)
    ms = (m0, m0)
    scores(0, sa_ref)
    if nk % 2 == 1:
        ms = lax.fori_loop(0, (nk - 1) // 2, pair, ms)
        process(nk - 1, sa_ref, ms)
    else:
        ms = lax.fori_loop(0, nk // 2 - 1, pair, ms)
        scores(nk - 1, sb_ref)
        ms = process(nk - 2, sa_ref, ms)
        process(nk - 1, sb_ref, ms)


def _da_attn_kernel(lam_ref, gain_ref, qt_ref, k_ref, vt_ref, o_ref, acc1_ref, acc2_ref, sa_ref, sb_ref, *,
                    nk, tk, c, out_scale):
    qt = qt_ref[0]
    sub = lax.broadcasted_iota(jnp.int32, (LANES, 1), 0)
    zero = jnp.zeros_like(qt)
    q1 = jnp.where(sub < DA_HEAD_DIM, qt, zero)
    q2 = jnp.where(sub >= DA_HEAD_DIM, qt, zero)
    full = slice(None)
    _attn_t_pipeline((q1, q2), (full, full), (full, full), k_ref, vt_ref, (acc1_ref, acc2_ref), sa_ref, sb_ref,
                     nk=nk, tk=tk, c=c)
    a1 = acc1_ref[...]
    a2 = acc2_ref[...]
    dv = LANES
    o = a1[:dv] / a1[dv:dv + 1] - lam_ref[0] * (a2[:dv] / a2[dv:dv + 1])
    o = o.T
    y = o * lax.rsqrt(jnp.mean(o * o, axis=-1, keepdims=True) + EPS)
    o_ref[0] = (y * gain_ref[...] * out_scale).astype(o_ref.dtype)


def _mla_attn_kernel(qt_ref, k_ref, vt_ref, o_ref, acc1_ref, acc2_ref, sa_ref, sb_ref, *, nk, tk, c):
    qa = qt_ref[0, :LANES, :]
    qb = qt_ref[0, LANES:, :]
    vr = MLA_V + ONES_ROWS
    _attn_t_pipeline((qa, qb), (slice(0, LANES), slice(LANES, 2 * LANES)), (slice(0, vr), slice(vr, 2 * vr)),
                     k_ref, vt_ref, (acc1_ref, acc2_ref), sa_ref, sb_ref, nk=nk, tk=tk, c=c)
    a1 = acc1_ref[...]
    a2 = acc2_ref[...]
    o = jnp.concatenate([a1[:MLA_V] / a1[MLA_V:MLA_V + 1], a2[:MLA_V] / a2[MLA_V:MLA_V + 1]], axis=0)
    o_ref[0] = o.T.astype(o_ref.dtype)


def _pick_tk(t):
    return 1408 if t % 1408 == 0 else 256


def _attn_call(kernel, qt, k, vt, extra, *, acc_rows, q0, nq, tq, k0, nk, tk, name):
    b, _, t = qt.shape
    g = vt.shape[1]
    qr = qt.shape[1] // g
    kl = k.shape[2] // g
    vr = vt.shape[2]
    body = functools.partial(kernel, nk=nk, tk=tk)
    extra_specs = [pl.BlockSpec(memory_space=pltpu.SMEM) if e.ndim == 1 else
                   pl.BlockSpec(e.shape, lambda bi, gi, qi: (0, 0)) for e in extra]
    return pl.pallas_call(
        body,
        grid=(b, g, nq),
        in_specs=extra_specs + [
            pl.BlockSpec((1, qr, tq), lambda bi, gi, qi: (bi, gi, q0 + qi)),
            pl.BlockSpec((1, nk * tk, kl), lambda bi, gi, qi: (bi, k0, gi)),
            pl.BlockSpec((1, 1, vr, nk * tk), lambda bi, gi, qi: (bi, gi, 0, k0)),
        ],
        out_specs=pl.BlockSpec((1, tq, LANES), lambda bi, gi, qi: (bi, qi, gi)),
        out_shape=jax.ShapeDtypeStruct((b, nq * tq, g * LANES), BF16),
        scratch_shapes=[pltpu.VMEM((acc_rows, tq), F32), pltpu.VMEM((acc_rows, tq), F32),
                        pltpu.VMEM((2, tk, tq), F32), pltpu.VMEM((2, tk, tq), F32)],
        compiler_params=pltpu.CompilerParams(
            dimension_semantics=("parallel", "parallel", "arbitrary"), vmem_limit_bytes=VMEM_LIMIT),
        name=name,
    )(*extra, qt, k, vt)


def _attention(kernel, q, k, v, extra, n_lat, dv, name):
    b, t, _ = q.shape
    n_ctx = t - n_lat
    qt = jnp.swapaxes(q, 1, 2)
    h = v.shape[2] // dv
    vt = jnp.swapaxes(v, 1, 2).reshape(b, h, dv, t)
    vt = jnp.concatenate([vt, jnp.ones((b, h, ONES_ROWS, t), vt.dtype)], axis=2)
    g = v.shape[2] // LANES
    vt = vt.reshape(b, g, -1, t)
    tk = _pick_tk(t)
    tq = min(ATTN_TQ, n_lat)
    acc_rows = dv + ONES_ROWS
    o_lat = _attn_call(kernel, qt, k, vt, extra, acc_rows=acc_rows, q0=0, nq=n_lat // tq, tq=tq, k0=0,
                       nk=t // tk, tk=tk, name=name + "_lat")
    o_ctx = _attn_call(kernel, qt, k, vt, extra, acc_rows=acc_rows, q0=n_lat // n_ctx, nq=1, tq=n_ctx,
                       k0=n_lat // n_ctx, nk=1, tk=n_ctx, name=name + "_ctx")
    return jnp.concatenate([o_lat, o_ctx], axis=1)


def _rms_norm(x, g):
    xf = x.astype(F32)
    y = xf * lax.rsqrt(jnp.mean(xf * xf, axis=-1, keepdims=True) + EPS)
    return y * g.astype(F32)


def _l2_norm(x):
    return x * lax.rsqrt(jnp.sum(x * x, axis=-1, keepdims=True) + EPS)


def _rope_tables(rows, dim):
    nf = dim // 4
    inv = ROPE_BASE ** (-jnp.arange(nf, dtype=F32) / nf)
    r = jnp.repeat(jnp.arange(rows, dtype=F32), GRID_W)
    c = jnp.tile(jnp.arange(GRID_W, dtype=F32), rows)
    ang = jnp.concatenate([r[:, None] * inv, c[:, None] * inv], axis=-1)
    return jnp.cos(ang), jnp.sin(ang)


def _apply_rope(x, cos, sin):
    half = x.shape[-1] // 2
    x1, x2 = x[..., :half], x[..., half:]
    cs, sn = cos[:, None, :], sin[:, None, :]
    return jnp.concatenate([x1 * cs - x2 * sn, x2 * cs + x1 * sn], axis=-1)


def _diff_attention(hc, hl, w_in, q_gain, k_gain, lam_vecs, sub_gain, w_out, lam_init, cos, sin):
    lv = lam_vecs.astype(F32)
    lam = jnp.exp(jnp.sum(lv[0] * lv[1])) - jnp.exp(jnp.sum(lv[2] * lv[3])) + lam_init
    c = DA_HEAD_DIM ** -0.5 * LOG2E

    def project(h, rope):
        b, n, _ = h.shape
        q, k, v = jnp.split(h @ w_in, 3, axis=-1)
        q = _rms_norm(q.reshape(b, n, 2 * DA_HEADS, DA_HEAD_DIM), q_gain)
        k = _rms_norm(k.reshape(b, n, 2 * DA_HEADS, DA_HEAD_DIM), k_gain)
        if rope:
            q = _apply_rope(q, cos, sin)
            k = _apply_rope(k, cos, sin)
        return q.reshape(b, n, -1), k.reshape(b, n, -1), v

    ql, kl, vl = project(hl, True)
    qc, kc, vc = project(hc, False)
    q = jnp.concatenate([ql, qc], axis=1).astype(BF16)
    k = jnp.concatenate([kl, kc], axis=1).astype(BF16)
    v = jnp.concatenate([vl, vc], axis=1).astype(BF16)
    n = hl.shape[1]
    kern = functools.partial(_da_attn_kernel, c=c, out_scale=1.0 - lam_init)
    o = _attention(kern, q, k, v, [lam.reshape(1), sub_gain.reshape(1, -1).astype(F32)], n, 2 * DA_HEAD_DIM, "da")
    out = o.astype(F32) @ w_out
    return out[:, n:], out[:, :n]


def _mla_attention(hc, hl, w_down, q_a_gain, kv_a_gain, w_uq, w_ukv, q_gain, k_gain, w_out, cos, sin):
    c = MLA_QK ** -0.5 * LOG2E
    pad = LANES - MLA_QK

    def project(h, rope):
        b, n, _ = h.shape
        lat = h @ w_down
        cq = lat[..., :MLA_Q_RANK]
        ckv = lat[..., MLA_Q_RANK:MLA_Q_RANK + MLA_KV_RANK]
        kr = lat[..., MLA_Q_RANK + MLA_KV_RANK:]
        q = (_rms_norm(cq, q_a_gain) @ w_uq).reshape(b, n, MLA_HEADS, MLA_QK)
        kv = (_rms_norm(ckv, kv_a_gain) @ w_ukv).reshape(b, n, MLA_HEADS, MLA_NOPE + MLA_V)
        k = jnp.concatenate([kv[..., :MLA_NOPE],
                             jnp.broadcast_to(kr[:, :, None, :], (b, n, MLA_HEADS, MLA_ROPE))], axis=-1)
        v = kv[..., MLA_NOPE:]
        q = _rms_norm(q, q_gain)
        k = _rms_norm(k, k_gain)
        if rope:
            q = jnp.concatenate([q[..., :MLA_NOPE], _apply_rope(q[..., MLA_NOPE:], cos, sin)], axis=-1)
            k = jnp.concatenate([k[..., :MLA_NOPE], _apply_rope(k[..., MLA_NOPE:], cos, sin)], axis=-1)
        q = jnp.pad(q, ((0, 0), (0, 0), (0, 0), (0, pad)))
        k = jnp.pad(k, ((0, 0), (0, 0), (0, 0), (0, pad)))
        return q.reshape(b, n, -1), k.reshape(b, n, -1), v.reshape(b, n, -1)

    ql, kl, vl = project(hl, True)
    qc, kc, vc = project(hc, False)
    q = jnp.concatenate([ql, qc], axis=1).astype(BF16)
    k = jnp.concatenate([kl, kc], axis=1).astype(BF16)
    v = jnp.concatenate([vl, vc], axis=1).astype(BF16)
    n = hl.shape[1]
    o = _attention(functools.partial(_mla_attn_kernel, c=c), q, k, v, [], n, MLA_V, "mla")
    out = o.astype(F32) @ w_out
    return out[:, n:], out[:, :n]


def _centred_depthwise_conv(x, w):
    k = w.shape[0]
    return lax.conv_general_dilated(x, w[:, None, :].astype(x.dtype), window_strides=(1,),
                                    padding=[((k - 1) // 2, k // 2)],
                                    dimension_numbers=('NWC', 'WIO', 'NWC'),
                                    feature_group_count=x.shape[-1])


def _gated_delta_chunked(q, k, v, g, beta, s0):
    b, n, h = q.shape[:3]
    c = GDN_CHUNK

    def to_chunks(t):
        t = t.astype(F32).reshape((b, n // c, c, h) + t.shape[3:])
        return t.transpose((1, 0, 3, 2) + tuple(range(4, t.ndim)))

    q, k, v, g, beta = to_chunks(q), to_chunks(k), to_chunks(v), to_chunks(g), to_chunks(beta)
    decay = jnp.cumsum(g, axis=-1)
    tril = jnp.tril(jnp.ones((c, c), bool))
    strict = jnp.tril(jnp.ones((c, c), bool), -1)
    diff = decay[..., :, None] - decay[..., None, :]
    gamma = jnp.where(tril, jnp.exp(jnp.where(tril, diff, 0.0)), 0.0)
    kb = k * beta[..., None]
    vb = v * beta[..., None]
    a_mat = jnp.where(strict, jnp.einsum('...id,...jd->...ij', kb, k) * gamma, 0.0)
    eye = jnp.eye(c, dtype=F32)
    rhs = jnp.concatenate([vb, kb * jnp.exp(decay)[..., None]], axis=-1)
    sol = lax.linalg.triangular_solve(a_mat + eye, rhs, left_side=True, lower=True, unit_diagonal=True)
    dv = v.shape[-1]
    u, w = sol[..., :dv], sol[..., dv:]
    qk = jnp.where(tril, jnp.einsum('...id,...jd->...ij', q, k) * gamma, 0.0)
    q_dec = q * jnp.exp(decay)[..., None]
    k_dec = k * jnp.exp(decay[..., -1:] - decay)[..., None]
    last = jnp.exp(decay[..., -1])

    def step(s, inp):
        qd, kd, ww, uu, qkc, lst = inp
        v_new = uu - jnp.einsum('bhck,bhkv->bhcv', ww, s)
        o = jnp.einsum('bhck,bhkv->bhcv', qd, s) + jnp.einsum('bhcj,bhjv->bhcv', qkc, v_new)
        s = s * lst[..., None, None] + jnp.einsum('bhck,bhcv->bhkv', kd, v_new)
        return s, o

    s_fin, o = lax.scan(step, s0.astype(F32), (q_dec, k_dec, w, u, qk, last))
    o = o.transpose(1, 0, 3, 2, 4).reshape(b, n, h, dv)
    return o, s_fin


def _gated_deltanet(hc, hl, w_in, conv_w, a_log, dt_bias, o_gain, w_out, need_ctx):
    nqk = GDN_HEADS * GDN_DK
    nv = GDN_HEADS * GDN_DV
    nqkv = 2 * nqk + nv

    def prep(h):
        b, n, _ = h.shape
        p = h @ w_in
        qkv = jax.nn.silu(_centred_depthwise_conv(p[..., :nqkv], conv_w))
        q = _l2_norm(qkv[..., :nqk].reshape(b, n, GDN_HEADS, GDN_DK)) * (GDN_DK ** -0.5)
        k = _l2_norm(qkv[..., nqk:2 * nqk].reshape(b, n, GDN_HEADS, GDN_DK))
        v = qkv[..., 2 * nqk:].reshape(b, n, GDN_HEADS, GDN_DV)
        z = p[..., nqkv:nqkv + nv].reshape(b, n, GDN_HEADS, GDN_DV)
        ab = p[..., nqkv + nv:].astype(F32).reshape(b, n, 2, 2, GDN_HEADS)
        g = -jnp.exp(a_log.astype(F32)) * jax.nn.softplus(ab[:, :, 0] + dt_bias.astype(F32))
        beta = jax.nn.sigmoid(ab[:, :, 1])
        return (q, k, v, g, beta), z

    def scan_dir(t, d, s0):
        q, k, v, g, beta = t
        g, beta = g[:, :, d], beta[:, :, d]
        if d == 1:
            q, k, v, g, beta = [jnp.flip(a, axis=1) for a in (q, k, v, g, beta)]
        o, s = _gated_delta_chunked(q, k, v, g, beta, s0)
        if d == 1:
            o = jnp.flip(o, axis=1)
        return o, s

    tc, zc = prep(hc)
    tl, zl = prep(hl)
    s0 = jnp.zeros((hl.shape[0], GDN_HEADS, GDN_DK, GDN_DV), F32)
    oc_f, sc_f = scan_dir(tc, 0, s0)
    ol_f, _ = scan_dir(tl, 0, sc_f)
    oc_b, sc_b = scan_dir(tc, 1, s0)
    ol_b, _ = scan_dir(tl, 1, sc_b)

    def finish(o, z):
        b, n = o.shape[:2]
        o = _rms_norm(o, o_gain) * jax.nn.silu(z)
        return o.reshape(b, n, -1) @ w_out

    out_l = finish(ol_f + ol_b, zl)
    out_c = finish(oc_f + oc_b, zc) if need_ctx else None
    return out_c, out_l


def _expert_choice_ffn(h, w_router, w_gate, w_up, w_down):
    b, n, _ = h.shape
    cap = EC_FACTOR * n // N_EXPERTS
    aff = jax.nn.softmax((h @ w_router).astype(F32), axis=-1)
    gate, idx = lax.top_k(aff.transpose(0, 2, 1), cap)
    bi = jnp.arange(b)[:, None, None]
    xs = h[bi, idx]
    hg = jnp.einsum('becd,edf->becf', xs, w_gate)
    hu = jnp.einsum('becd,edf->becf', xs, w_up)
    y = jnp.einsum('becf,efd->becd', jax.nn.silu(hg) * hu, w_down) * gate[..., None].astype(h.dtype)
    return jnp.zeros_like(h).at[bi, idx].add(y)


def kernel(x, c, ctx, c_ctx, ada_w, ada_b, norm_g, da_w_in, da_q_gain, da_k_gain, da_lambda, da_sub_gain, da_w_out, mla_w_down, mla_q_a_gain, mla_kv_a_gain, mla_w_uq, mla_w_ukv, mla_q_gain, mla_k_gain, mla_w_out, gdn_w_in, gdn_conv_w, gdn_a_log, gdn_dt_bias, gdn_o_gain, gdn_w_out, moe_router, moe_w_gate, moe_w_up, moe_w_down):
    depth = ada_w.shape[0]
    rows = x.shape[1] // GRID_W
    da_cos, da_sin = _rope_tables(rows, DA_HEAD_DIM)
    mla_cos, mla_sin = _rope_tables(rows, MLA_ROPE)
    cs = ctx
    sc = jax.nn.silu(c)
    scc = jax.nn.silu(c_ctx)
    for l in range(depth):
        last = l == depth - 1
        j = l // N_MIXERS
        kind = l % N_MIXERS
        mod_l = (sc @ ada_w[l] + ada_b[l])[:, None, :]
        mod_c = scc @ ada_w[l] + ada_b[l]
        sh1l, sc1l, g1l, sh2l, sc2l, g2l = jnp.split(mod_l, 6, axis=-1)
        sh1c, sc1c, g1c, sh2c, sc2c, g2c = jnp.split(mod_c, 6, axis=-1)
        hl = _rms_norm(x, norm_g[l, 0]) * (1.0 + sc1l) + sh1l
        hc = _rms_norm(cs, norm_g[l, 0]) * (1.0 + sc1c) + sh1c
        if kind == 0:
            lam_init = 0.8 - 0.6 * math.exp(-0.3 * l)
            mc, ml = _diff_attention(hc, hl, da_w_in[j], da_q_gain[j], da_k_gain[j], da_lambda[j],
                                     da_sub_gain[j], da_w_out[j], lam_init, da_cos, da_sin)
        elif kind == 1:
            mc, ml = _mla_attention(hc, hl, mla_w_down[j], mla_q_a_gain[j], mla_kv_a_gain[j], mla_w_uq[j],
                                    mla_w_ukv[j], mla_q_gain[j], mla_k_gain[j], mla_w_out[j], mla_cos, mla_sin)
        else:
            mc, ml = _gated_deltanet(hc, hl, gdn_w_in[j], gdn_conv_w[j], gdn_a_log[j], gdn_dt_bias[j],
                                     gdn_o_gain[j], gdn_w_out[j], not last)
        x = x + g1l * ml
        hl2 = _rms_norm(x, norm_g[l, 1]) * (1.0 + sc2l) + sh2l
        x = x + g2l * _expert_choice_ffn(hl2, moe_router[l], moe_w_gate[l], moe_w_up[l], moe_w_down[l])
        if not last:
            cs = cs + g1c * mc
            hc2 = _rms_norm(cs, norm_g[l, 1]) * (1.0 + sc2c) + sh2c
            cs = cs + g2c * _expert_choice_ffn(hc2, moe_router[l], moe_w_gate[l], moe_w_up[l], moe_w_down[l])
    return x
```

```python
import functools
import math

import jax
import jax.numpy as jnp
from jax import lax
from jax.experimental import pallas as pl
from jax.experimental.pallas import tpu as pltpu

F32 = jnp.float32
BF16 = jnp.bfloat16

GRID_W = 64
N_MIXERS = 3
ROPE_BASE = 10000.0
EPS = 1e-6

DA_HEADS = 8
DA_HEAD_DIM = 64
MLA_HEADS = 16
MLA_Q_RANK = 256
MLA_KV_RANK = 128
MLA_NOPE = 64
MLA_ROPE = 32
MLA_V = 64
MLA_QK = MLA_NOPE + MLA_ROPE
GDN_HEADS = 8
GDN_DK = 128
GDN_DV = 128
GDN_CHUNK = 64
N_EXPERTS = 16
EC_FACTOR = 2

LANES = 128
ATTN_TQ = 512
LOG2E = math.log2(math.e)
NEG_INIT = -1e30
VMEM_LIMIT = 56 * 1024 * 1024


ONES_ROWS = 8


def _attn_t_pipeline(q_maps, k_lanes, v_rows, k_ref, vt_ref, acc_refs, sa_ref, sb_ref, *, nk, tk, c):
    tq = q_maps[0].shape[1]
    for acc in acc_refs:
        acc[...] = jnp.zeros_like(acc)

    def scores(j, s_out):
        off = pl.multiple_of(j * tk, tk)
        for mp in range(2):
            kk = k_ref[0, pl.ds(off, tk), k_lanes[mp]]
            s_out[mp] = jnp.dot(kk, q_maps[mp], preferred_element_type=F32)

    def process(j, s_in, ms):
        off = pl.multiple_of(j * tk, tk)
        out = []
        for mp in range(2):
            s = s_in[mp]
            m_new = jnp.maximum(ms[mp], jnp.max(s, axis=0, keepdims=True))
            alpha = jnp.exp2((ms[mp] - m_new) * c)
            p = jnp.exp2((s - m_new) * c).astype(BF16)
            vv = vt_ref[0, 0, v_rows[mp], pl.ds(off, tk)]
            acc_refs[mp][...] = alpha * acc_refs[mp][...] + jnp.dot(vv, p, preferred_element_type=F32)
            out.append(m_new)
        return tuple(out)

    def pair(i, ms):
        j = 2 * i
        scores(j + 1, sb_ref)
        ms = process(j, sa_ref, ms)
        scores(j + 2, sa_ref)
        return process(j + 1, sb_ref, ms)

    m0 = jnp.full((1, tq), NEG_INIT, F32)
    ms = (m0, m0)
    scores(0, sa_ref)
    if nk % 2 == 1:
        ms = lax.fori_loop(0, (nk - 1) // 2, pair, ms)
        process(nk - 1, sa_ref, ms)
    else:
        ms = lax.fori_loop(0, nk // 2 - 1, pair, ms)
        scores(nk - 1, sb_ref)
        ms = process(nk - 2, sa_ref, ms)
        process(nk - 1, sb_ref, ms)


def _da_attn_kernel(lam_ref, gain_ref, qt_ref, k_ref, vt_ref, o_ref, acc1_ref, acc2_ref, sa_ref, sb_ref, *,
                    nk, tk, c, out_scale):
    qt = qt_ref[0]
    sub = lax.broadcasted_iota(jnp.int32, (LANES, 1), 0)
    zero = jnp.zeros_like(qt)
    q1 = jnp.where(sub < DA_HEAD_DIM, qt, zero)
    q2 = jnp.where(sub >= DA_HEAD_DIM, qt, zero)
    full = slice(None)
    _attn_t_pipeline((q1, q2), (full, full), (full, full), k_ref, vt_ref, (acc1_ref, acc2_ref), sa_ref, sb_ref,
                     nk=nk, tk=tk, c=c)
    a1 = acc1_ref[...]
    a2 = acc2_ref[...]
    dv = LANES
    o = a1[:dv] / a1[dv:dv + 1] - lam_ref[0] * (a2[:dv] / a2[dv:dv + 1])
    o = o.T
    y = o * lax.rsqrt(jnp.mean(o * o, axis=-1, keepdims=True) + EPS)
    o_ref[0] = (y * gain_ref[...] * out_scale).astype(o_ref.dtype)


def _mla_attn_kernel(qt_ref, k_ref, vt_ref, o_ref, acc1_ref, acc2_ref, sa_ref, sb_ref, *, nk, tk, c):
    qa = qt_ref[0, :LANES, :]
    qb = qt_ref[0, LANES:, :]
    vr = MLA_V + ONES_ROWS
    _attn_t_pipeline((qa, qb), (slice(0, LANES), slice(LANES, 2 * LANES)), (slice(0, vr), slice(vr, 2 * vr)),
                     k_ref, vt_ref, (acc1_ref, acc2_ref), sa_ref, sb_ref, nk=nk, tk=tk, c=c)
    a1 = acc1_ref[...]
    a2 = acc2_ref[...]
    o = jnp.concatenate([a1[:MLA_V] / a1[MLA_V:MLA_V + 1], a2[:MLA_V] / a2[MLA_V:MLA_V + 1]], axis=0)
    o_ref[0] = o.T.astype(o_ref.dtype)


def _pick_tk(t):
    return 1408 if t % 1408 == 0 else 256


def _attn_call(kernel, qt, k, vt, extra, *, acc_rows, q0, nq, tq, k0, nk, tk, name):
    b, _, t = qt.shape
    g = vt.shape[1]
    qr = qt.shape[1] // g
    kl = k.shape[2] // g
    vr = vt.shape[2]
    body = functools.partial(kernel, nk=nk, tk=tk)
    extra_specs = [pl.BlockSpec(memory_space=pltpu.SMEM) if e.ndim == 1 else
                   pl.BlockSpec(e.shape, lambda bi, gi, qi: (0, 0)) for e in extra]
    return pl.pallas_call(
        body,
        grid=(b, g, nq),
        in_specs=extra_specs + [
            pl.BlockSpec((1, qr, tq), lambda bi, gi, qi: (bi, gi, q0 + qi)),
            pl.BlockSpec((1, nk * tk, kl), lambda bi, gi, qi: (bi, k0, gi)),
            pl.BlockSpec((1, 1, vr, nk * tk), lambda bi, gi, qi: (bi, gi, 0, k0)),
        ],
        out_specs=pl.BlockSpec((1, tq, LANES), lambda bi, gi, qi: (bi, qi, gi)),
        out_shape=jax.ShapeDtypeStruct((b, nq * tq, g * LANES), BF16),
        scratch_shapes=[pltpu.VMEM((acc_rows, tq), F32), pltpu.VMEM((acc_rows, tq), F32),
                        pltpu.VMEM((2, tk, tq), F32), pltpu.VMEM((2, tk, tq), F32)],
        compiler_params=pltpu.CompilerParams(
            dimension_semantics=("parallel", "parallel", "arbitrary"), vmem_limit_bytes=VMEM_LIMIT),
        name=name,
    )(*extra, qt, k, vt)


def _attention(kernel, q, k, v, extra, n_lat, dv, name):
    b, t, _ = q.shape
    n_ctx = t - n_lat
    qt = jnp.swapaxes(q, 1, 2)
    h = v.shape[2] // dv
    vt = jnp.swapaxes(v, 1, 2).reshape(b, h, dv, t)
    vt = jnp.concatenate([vt, jnp.ones((b, h, ONES_ROWS, t), vt.dtype)], axis=2)
    g = v.shape[2] // LANES
    vt = vt.reshape(b, g, -1, t)
    tk = _pick_tk(t)
    tq = min(ATTN_TQ, n_lat)
    acc_rows = dv + ONES_ROWS
    o_lat = _attn_call(kernel, qt, k, vt, extra, acc_rows=acc_rows, q0=0, nq=n_lat // tq, tq=tq, k0=0,
                       nk=t // tk, tk=tk, name=name + "_lat")
    o_ctx = _attn_call(kernel, qt, k, vt, extra, acc_rows=acc_rows, q0=n_lat // n_ctx, nq=1, tq=n_ctx,
                       k0=n_lat // n_ctx, nk=1, tk=n_ctx, name=name + "_ctx")
    return jnp.concatenate([o_lat, o_ctx], axis=1)


def _rms_norm(x, g):
    xf = x.astype(F32)
    y = xf * lax.rsqrt(jnp.mean(xf * xf, axis=-1, keepdims=True) + EPS)
    return y * g.astype(F32)


def _l2_norm(x):
    return x * lax.rsqrt(jnp.sum(x * x, axis=-1, keepdims=True) + EPS)


def _rope_tables(rows, dim):
    nf = dim // 4
    inv = ROPE_BASE ** (-jnp.arange(nf, dtype=F32) / nf)
    r = jnp.repeat(jnp.arange(rows, dtype=F32), GRID_W)
    c = jnp.tile(jnp.arange(GRID_W, dtype=F32), rows)
    ang = jnp.concatenate([r[:, None] * inv, c[:, None] * inv], axis=-1)
    return jnp.cos(ang), jnp.sin(ang)


def _apply_rope(x, cos, sin):
    half = x.shape[-1] // 2
    x1, x2 = x[..., :half], x[..., half:]
    cs, sn = cos[:, None, :], sin[:, None, :]
    return jnp.concatenate([x1 * cs - x2 * sn, x2 * cs + x1 * sn], axis=-1)


def _diff_attention(hc, hl, w_in, q_gain, k_gain, lam_vecs, sub_gain, w_out, lam_init, cos, sin):
    lv = lam_vecs.astype(F32)
    lam = jnp.exp(jnp.sum(lv[0] * lv[1])) - jnp.exp(jnp.sum(lv[2] * lv[3])) + lam_init
    c = DA_HEAD_DIM ** -0.5 * LOG2E

    def project(h, rope):
        b, n, _ = h.shape
        q, k, v = jnp.split(h @ w_in, 3, axis=-1)
        q = _rms_norm(q.reshape(b, n, 2 * DA_HEADS, DA_HEAD_DIM), q_gain)
        k = _rms_norm(k.reshape(b, n, 2 * DA_HEADS, DA_HEAD_DIM), k_gain)
        if rope:
            q = _apply_rope(q, cos, sin)
            k = _apply_rope(k, cos, sin)
        return q.reshape(b, n, -1), k.reshape(b, n, -1), v

    ql, kl, vl = project(hl, True)
    qc, kc, vc = project(hc, False)
    q = jnp.concatenate([ql, qc], axis=1).astype(BF16)
    k = jnp.concatenate([kl, kc], axis=1).astype(BF16)
    v = jnp.concatenate([vl, vc], axis=1).astype(BF16)
    n = hl.shape[1]
    kern = functools.partial(_da_attn_kernel, c=c, out_scale=1.0 - lam_init)
    o = _attention(kern, q, k, v, [lam.reshape(1), sub_gain.reshape(1, -1).astype(F32)], n, 2 * DA_HEAD_DIM, "da")
    out = o.astype(F32) @ w_out
    return out[:, n:], out[:, :n]


def _mla_attention(hc, hl, w_down, q_a_gain, kv_a_gain, w_uq, w_ukv, q_gain, k_gain, w_out, cos, sin):
    c = MLA_QK ** -0.5 * LOG2E
    pad = LANES - MLA_QK

    def project(h, rope):
        b, n, _ = h.shape
        lat = h @ w_down
        cq = lat[..., :MLA_Q_RANK]
        ckv = lat[..., MLA_Q_RANK:MLA_Q_RANK + MLA_KV_RANK]
        kr = lat[..., MLA_Q_RANK + MLA_KV_RANK:]
        q = (_rms_norm(cq, q_a_gain) @ w_uq).reshape(b, n, MLA_HEADS, MLA_QK)
        kv = (_rms_norm(ckv, kv_a_gain) @ w_ukv).reshape(b, n, MLA_HEADS, MLA_NOPE + MLA_V)
        k = jnp.concatenate([kv[..., :MLA_NOPE],
                             jnp.broadcast_to(kr[:, :, None, :], (b, n, MLA_HEADS, MLA_ROPE))], axis=-1)
        v = kv[..., MLA_NOPE:]
        q = _rms_norm(q, q_gain)
        k = _rms_norm(k, k_gain)
        if rope:
            q = jnp.concatenate([q[..., :MLA_NOPE], _apply_rope(q[..., MLA_NOPE:], cos, sin)], axis=-1)
            k = jnp.concatenate([k[..., :MLA_NOPE], _apply_rope(k[..., MLA_NOPE:], cos, sin)], axis=-1)
        q = jnp.pad(q, ((0, 0), (0, 0), (0, 0), (0, pad)))
        k = jnp.pad(k, ((0, 0), (0, 0), (0, 0), (0, pad)))
        return q.reshape(b, n, -1), k.reshape(b, n, -1), v.reshape(b, n, -1)

    ql, kl, vl = project(hl, True)
    qc, kc, vc = project(hc, False)
    q = jnp.concatenate([ql, qc], axis=1).astype(BF16)
    k = jnp.concatenate([kl, kc], axis=1).astype(BF16)
    v = jnp.concatenate([vl, vc], axis=1).astype(BF16)
    n = hl.shape[1]
    o = _attention(functools.partial(_mla_attn_kernel, c=c), q, k, v, [], n, MLA_V, "mla")
    out = o.astype(F32) @ w_out
    return out[:, n:], out[:, :n]


def _centred_depthwise_conv(x, w):
    k = w.shape[0]
    return lax.conv_general_dilated(x, w[:, None, :].astype(x.dtype), window_strides=(1,),
                                    padding=[((k - 1) // 2, k // 2)],
                                    dimension_numbers=('NWC', 'WIO', 'NWC'),
                                    feature_group_count=x.shape[-1])


def _gated_delta_chunked(q, k, v, g, beta, s0):
    b, n, h = q.shape[:3]
    c = GDN_CHUNK

    def to_chunks(t):
        t = t.astype(F32).reshape((b, n // c, c, h) + t.shape[3:])
        return t.transpose((1, 0, 3, 2) + tuple(range(4, t.ndim)))

    q, k, v, g, beta = to_chunks(q), to_chunks(k), to_chunks(v), to_chunks(g), to_chunks(beta)
    decay = jnp.cumsum(g, axis=-1)
    tril = jnp.tril(jnp.ones((c, c), bool))
    strict = jnp.tril(jnp.ones((c, c), bool), -1)
    diff = decay[..., :, None] - decay[..., None, :]
    gamma = jnp.where(tril, jnp.exp(jnp.where(tril, diff, 0.0)), 0.0)
    kb = k * beta[..., None]
    vb = v * beta[..., None]
    a_mat = jnp.where(strict, jnp.einsum('...id,...jd->...ij', kb, k) * gamma, 0.0)
    eye = jnp.eye(c, dtype=F32)
    rhs = jnp.concatenate([vb, kb * jnp.exp(decay)[..., None]], axis=-1)
    sol = lax.linalg.triangular_solve(a_mat + eye, rhs, left_side=True, lower=True, unit_diagonal=True)
    dv = v.shape[-1]
    u, w = sol[..., :dv], sol[..., dv:]
    qk = jnp.where(tril, jnp.einsum('...id,...jd->...ij', q, k) * gamma, 0.0)
    q_dec = q * jnp.exp(decay)[..., None]
    k_dec = k * jnp.exp(decay[..., -1:] - decay)[..., None]
    last = jnp.exp(decay[..., -1])

    def step(s, inp):
        qd, kd, ww, uu, qkc, lst = inp
        v_new = uu - jnp.einsum('bhck,bhkv->bhcv', ww, s)
        o = jnp.einsum('bhck,bhkv->bhcv', qd, s) + jnp.einsum('bhcj,bhjv->bhcv', qkc, v_new)
        s = s * lst[..., None, None] + jnp.einsum('bhck,bhcv->bhkv', kd, v_new)
        return s, o

    s_fin, o = lax.scan(step, s0.astype(F32), (q_dec, k_dec, w, u, qk, last))
    o = o.transpose(1, 0, 3, 2, 4).reshape(b, n, h, dv)
    return o, s_fin


def _gated_deltanet(hc, hl, w_in, conv_w, a_log, dt_bias, o_gain, w_out, need_ctx):
    nqk = GDN_HEADS * GDN_DK
    nv = GDN_HEADS * GDN_DV
    nqkv = 2 * nqk + nv

    def prep(h):
        b, n, _ = h.shape
        p = h @ w_in
        qkv = jax.nn.silu(_centred_depthwise_conv(p[..., :nqkv], conv_w))
        q = _l2_norm(qkv[..., :nqk].reshape(b, n, GDN_HEADS, GDN_DK)) * (GDN_DK ** -0.5)
        k = _l2_norm(qkv[..., nqk:2 * nqk].reshape(b, n, GDN_HEADS, GDN_DK))
        v = qkv[..., 2 * nqk:].reshape(b, n, GDN_HEADS, GDN_DV)
        z = p[..., nqkv:nqkv + nv].reshape(b, n, GDN_HEADS, GDN_DV)
        ab = p[..., nqkv + nv:].astype(F32).reshape(b, n, 2, 2, GDN_HEADS)
        g = -jnp.exp(a_log.astype(F32)) * jax.nn.softplus(ab[:, :, 0] + dt_bias.astype(F32))
        beta = jax.nn.sigmoid(ab[:, :, 1])
        return (q, k, v, g, beta), z

    def scan_dir(t, d, s0):
        q, k, v, g, beta = t
        g, beta = g[:, :, d], beta[:, :, d]
        if d == 1:
            q, k, v, g, beta = [jnp.flip(a, axis=1) for a in (q, k, v, g, beta)]
        o, s = _gated_delta_chunked(q, k, v, g, beta, s0)
        if d == 1:
            o = jnp.flip(o, axis=1)
        return o, s

    tc, zc = prep(hc)
    tl, zl = prep(hl)
    s0 = jnp.zeros((hl.shape[0], GDN_HEADS, GDN_DK, GDN_DV), F32)
    oc_f, sc_f = scan_dir(tc, 0, s0)
    ol_f, _ = scan_dir(tl, 0, sc_f)
    oc_b, sc_b = scan_dir(tc, 1, s0)
    ol_b, _ = scan_dir(tl, 1, sc_b)

    def finish(o, z):
        b, n = o.shape[:2]
        o = _rms_norm(o, o_gain) * jax.nn.silu(z)
        return o.reshape(b, n, -1) @ w_out

    out_l = finish(ol_f + ol_b, zl)
    out_c = finish(oc_f + oc_b, zc) if need_ctx else None
    return out_c, out_l


ROUTE_BLK = 128
ROUTE_CHUNK = 256


def _route_kernel(aff_ref, idx_ref, cs_ref, sel_ref, *, cap):
    n, ne = aff_ref.shape[1], aff_ref.shape[2]
    bits = pltpu.bitcast(aff_ref[0], jnp.int32)

    def bisect(i, thr):
        cand = thr | jnp.left_shift(jnp.int32(1), 30 - i)
        cnt = jnp.sum((bits >= cand).astype(jnp.int32), axis=0, keepdims=True)
        return jnp.where(cnt >= cap, cand, thr)

    thr = lax.fori_loop(0, 31, bisect, jnp.zeros((1, ne), jnp.int32))
    gt = bits > thr
    eq = bits == thr
    need = cap - jnp.sum(gt.astype(jnp.int32), axis=0, keepdims=True)

    flags = jnp.concatenate([gt.astype(BF16), eq.astype(BF16)], axis=1)
    r = lax.broadcasted_iota(jnp.int32, (ROUTE_BLK, ROUTE_BLK), 0)
    cidx = lax.broadcasted_iota(jnp.int32, (ROUTE_BLK, ROUTE_BLK), 1)
    tri = (cidx <= r).astype(BF16)
    carry = jnp.zeros((1, 2 * ne), F32)
    for blk in range(n // ROUTE_BLK):
        rows = slice(blk * ROUTE_BLK, (blk + 1) * ROUTE_BLK)
        part = jnp.dot(tri, flags[rows], preferred_element_type=F32) + carry
        cs_ref[rows, :] = part
        carry = part[ROUTE_BLK - 1:ROUTE_BLK, :]
    cs = cs_ref[...]
    sel_ref[...] = cs[:, :ne] + jnp.minimum(cs[:, ne:], need.astype(F32))

    cb = min(cap, LANES)
    nb = cap // cb
    tc = min(n, ROUTE_CHUNK)
    slots = [(lax.broadcasted_iota(jnp.int32, (1, cb), 1) + jb * cb).astype(F32) for jb in range(nb)]
    for e in range(ne):
        def chunk(ci, accs, e=e):
            rows = pl.ds(pl.multiple_of(ci * tc, tc), tc)
            col = jnp.broadcast_to(sel_ref[rows, e:e + 1], (tc, cb))
            return tuple(acc + (col <= slot).astype(F32).reshape(tc // 8, 8, cb).sum(axis=0)
                         for acc, slot in zip(accs, slots))

        accs = lax.fori_loop(0, n // tc, chunk, tuple(jnp.zeros((8, cb), F32) for _ in range(nb)))
        for jb in range(nb):
            cnt = jnp.sum(accs[jb], axis=0, keepdims=True)
            idx_ref[0, e:e + 1, jb * cb:(jb + 1) * cb] = cnt.astype(jnp.int32)


def _route(aff, cap):
    b, n, ne = aff.shape
    cb = min(cap, LANES)
    return pl.pallas_call(
        functools.partial(_route_kernel, cap=cap),
        grid=(b,),
        in_specs=[pl.BlockSpec((1, n, ne), lambda bi: (bi, 0, 0))],
        out_specs=pl.BlockSpec((1, ne, cap), lambda bi: (bi, 0, 0)),
        out_shape=jax.ShapeDtypeStruct((b, ne, cap), jnp.int32),
        scratch_shapes=[pltpu.VMEM((n, 2 * ne), F32), pltpu.VMEM((n, ne), F32)],
        compiler_params=pltpu.CompilerParams(dimension_semantics=("parallel",), vmem_limit_bytes=VMEM_LIMIT),
        name="moe_route",
    )(aff)


MOE_FF_CHUNK = 512
GATHER_UNROLL = 8


def _moe_mlp_kernel(idx_ref, idx_next_ref, h_ref, gate_ref, wg_ref, wu_ref, wd_ref, y_ref, xbuf, sem, *, total):
    tm = xbuf.shape[1]
    s = pl.program_id(0) * pl.num_programs(1) + pl.program_id(1)
    slot = lax.rem(s, 2)

    def row_copy(src_row, dst_slot, i):
        return pltpu.make_async_copy(h_ref.at[pl.ds(src_row, 1), :], xbuf.at[dst_slot, pl.ds(i, 1), :],
                                     sem.at[dst_slot])

    def wait_gather(dst_slot):
        pltpu.make_async_copy(h_ref.at[pl.ds(0, tm), :], xbuf.at[dst_slot], sem.at[dst_slot]).wait()

    @pl.when(s == 0)
    def _():
        def body(i, carry):
            row_copy(idx_ref[0, 0, i], 0, i).start()
            return carry
        lax.fori_loop(0, tm, body, 0, unroll=GATHER_UNROLL)

    wait_gather(slot)
    x = xbuf[slot].astype(BF16)
    ff = wg_ref.shape[2]
    n_chunks = ff // MOE_FF_CHUNK
    per_chunk = tm // n_chunks
    y = jnp.zeros((tm, wd_ref.shape[2]), F32)
    for ci in range(n_chunks):
        cols = slice(ci * MOE_FF_CHUNK, (ci + 1) * MOE_FF_CHUNK)
        hg = jnp.dot(x, wg_ref[0, :, cols], preferred_element_type=F32)
        hu = jnp.dot(x, wu_ref[0, :, cols], preferred_element_type=F32)
        act = (jax.nn.silu(hg) * hu).astype(BF16)
        y = y + jnp.dot(act, wd_ref[0, cols, :], preferred_element_type=F32)
        for i in range(ci * per_chunk, (ci + 1) * per_chunk):
            row_copy(idx_next_ref[0, 0, i], 1 - slot, i).start()
    y_ref[...] = y * gate_ref[0]

    @pl.when(s == total - 1)
    def _():
        wait_gather(1 - slot)


def _pick_tile(n, candidates):
    for t in candidates:
        if n % t == 0:
            return t
    raise ValueError(f"no tile for {n}")


def _moe_mlp(h_rows, idx_all, gate_all, w_gate, w_up, w_down):
    ne, s_total = idx_all.shape
    d = h_rows.shape[1]
    ff = w_gate.shape[2]
    tm = _pick_tile(s_total, (512, 384, 256, 128, 96, 64, 32, 16, 8))
    nt = s_total // tm
    total = ne * nt
    idx3 = idx_all.reshape(total, 1, tm)
    gate3 = gate_all.reshape(total, tm, 1)
    return pl.pallas_call(
        functools.partial(_moe_mlp_kernel, total=total),
        grid=(ne, nt),
        in_specs=[
            pl.BlockSpec((1, 1, tm), lambda e, t: (e * nt + t, 0, 0), memory_space=pltpu.SMEM),
            pl.BlockSpec((1, 1, tm), lambda e, t: (jnp.minimum(e * nt + t + 1, total - 1), 0, 0),
                         memory_space=pltpu.SMEM),
            pl.BlockSpec(memory_space=pl.ANY),
            pl.BlockSpec((1, tm, 1), lambda e, t: (e * nt + t, 0, 0)),
            pl.BlockSpec((1, d, ff), lambda e, t: (e, 0, 0)),
            pl.BlockSpec((1, d, ff), lambda e, t: (e, 0, 0)),
            pl.BlockSpec((1, ff, d), lambda e, t: (e, 0, 0)),
        ],
        out_specs=pl.BlockSpec((tm, d), lambda e, t: (e * nt + t, 0)),
        out_shape=jax.ShapeDtypeStruct((total * tm, d), F32),
        scratch_shapes=[pltpu.VMEM((2, tm, d), F32), pltpu.SemaphoreType.DMA((2,))],
        compiler_params=pltpu.CompilerParams(
            dimension_semantics=("arbitrary", "arbitrary"), vmem_limit_bytes=VMEM_LIMIT),
        name="moe_mlp",
    )(idx3, idx3, h_rows, gate3, w_gate, w_up, w_down)


def _expert_choice_ffn(hs, w_router, w_gate, w_up, w_down):
    b = hs[0].shape[0]
    d = hs[0].shape[2]
    t = sum(h.shape[1] for h in hs)
    idx_parts, gate_parts = [], []
    base = 0
    for h in hs:
        n = h.shape[1]
        cap = EC_FACTOR * n // N_EXPERTS
        aff = jax.nn.softmax((h @ w_router).astype(F32), axis=-1)
        idx = _route(aff, cap)
        gate = jnp.take_along_axis(aff.transpose(0, 2, 1), idx, axis=2)
        idx_parts.append(idx + (jnp.arange(b, dtype=jnp.int32)[:, None, None] * t + base))
        gate_parts.append(gate)
        base += n
    idx_all = jnp.concatenate(idx_parts, axis=2).transpose(1, 0, 2).reshape(N_EXPERTS, -1)
    gate_all = jnp.concatenate(gate_parts, axis=2).transpose(1, 0, 2).reshape(N_EXPERTS, -1)
    h_rows = jnp.concatenate(hs, axis=1).reshape(b * t, d)
    y = _moe_mlp(h_rows, idx_all, gate_all, w_gate.astype(BF16), w_up.astype(BF16), w_down.astype(BF16))
    out = jnp.zeros((b * t, d), F32).at[idx_all.reshape(-1)].add(y).reshape(b, t, d)
    outs, base = [], 0
    for h in hs:
        outs.append(out[:, base:base + h.shape[1]])
        base += h.shape[1]
    return outs


def kernel(x, c, ctx, c_ctx, ada_w, ada_b, norm_g, da_w_in, da_q_gain, da_k_gain, da_lambda, da_sub_gain, da_w_out, mla_w_down, mla_q_a_gain, mla_kv_a_gain, mla_w_uq, mla_w_ukv, mla_q_gain, mla_k_gain, mla_w_out, gdn_w_in, gdn_conv_w, gdn_a_log, gdn_dt_bias, gdn_o_gain, gdn_w_out, moe_router, moe_w_gate, moe_w_up, moe_w_down):
    depth = ada_w.shape[0]
    rows = x.shape[1] // GRID_W
    da_cos, da_sin = _rope_tables(rows, DA_HEAD_DIM)
    mla_cos, mla_sin = _rope_tables(rows, MLA_ROPE)
    cs = ctx
    sc = jax.nn.silu(c)
    scc = jax.nn.silu(c_ctx)
    for l in range(depth):
        last = l == depth - 1
        j = l // N_MIXERS
        kind = l % N_MIXERS
        mod_l = (sc @ ada_w[l] + ada_b[l])[:, None, :]
        mod_c = scc @ ada_w[l] + ada_b[l]
        sh1l, sc1l, g1l, sh2l, sc2l, g2l = jnp.split(mod_l, 6, axis=-1)
        sh1c, sc1c, g1c, sh2c, sc2c, g2c = jnp.split(mod_c, 6, axis=-1)
        hl = _rms_norm(x, norm_g[l, 0]) * (1.0 + sc1l) + sh1l
        hc = _rms_norm(cs, norm_g[l, 0]) * (1.0 + sc1c) + sh1c
        if kind == 0:
            lam_init = 0.8 - 0.6 * math.exp(-0.3 * l)
            mc, ml = _diff_attention(hc, hl, da_w_in[j], da_q_gain[j], da_k_gain[j], da_lambda[j],
                                     da_sub_gain[j], da_w_out[j], lam_init, da_cos, da_sin)
        elif kind == 1:
            mc, ml = _mla_attention(hc, hl, mla_w_down[j], mla_q_a_gain[j], mla_kv_a_gain[j], mla_w_uq[j],
                                    mla_w_ukv[j], mla_q_gain[j], mla_k_gain[j], mla_w_out[j], mla_cos, mla_sin)
        else:
            mc, ml = _gated_deltanet(hc, hl, gdn_w_in[j], gdn_conv_w[j], gdn_a_log[j], gdn_dt_bias[j],
                                     gdn_o_gain[j], gdn_w_out[j], not last)
        x = x + g1l * ml
        hl2 = _rms_norm(x, norm_g[l, 1]) * (1.0 + sc2l) + sh2l
        streams = [hl2]
        if not last:
            cs = cs + g1c * mc
            streams.append(_rms_norm(cs, norm_g[l, 1]) * (1.0 + sc2c) + sh2c)
        ffn = _expert_choice_ffn(streams, moe_router[l], moe_w_gate[l], moe_w_up[l], moe_w_down[l])
        x = x + g2l * ffn[0]
        if not last:
            cs = cs + g2c * ffn[1]
    return x
```

```python
import functools
import math

import jax
import jax.numpy as jnp
from jax import lax
from jax.experimental import pallas as pl
from jax.experimental.pallas import tpu as pltpu

F32 = jnp.float32
BF16 = jnp.bfloat16

GRID_W = 64
N_MIXERS = 3
ROPE_BASE = 10000.0
EPS = 1e-6

DA_HEADS = 8
DA_HEAD_DIM = 64
MLA_HEADS = 16
MLA_Q_RANK = 256
MLA_KV_RANK = 128
MLA_NOPE = 64
MLA_ROPE = 32
MLA_V = 64
MLA_QK = MLA_NOPE + MLA_ROPE
GDN_HEADS = 8
GDN_DK = 128
GDN_DV = 128
GDN_CHUNK = 64
N_EXPERTS = 16
EC_FACTOR = 2

LANES = 128
ATTN_TQ = 512
LOG2E = math.log2(math.e)
NEG_INIT = -1e30
BOUND_SAFE = 60.0
VMEM_LIMIT = 56 * 1024 * 1024


ONES_ROWS = 8


def _attn_t_pipeline(q_maps, k_lanes, v_rows, k_ref, vt_ref, acc_refs, sa_ref, sb_ref, *, nk, tk, c, bounds=None):
    tq = q_maps[0].shape[1]
    for acc in acc_refs:
        acc[...] = jnp.zeros_like(acc)

    def scores(j, s_out):
        off = pl.multiple_of(j * tk, tk)
        for mp in range(2):
            kk = k_ref[0, pl.ds(off, tk), k_lanes[mp]]
            s_out[mp] = jnp.dot(kk, q_maps[mp], preferred_element_type=F32)

    def process(j, s_in, ms):
        off = pl.multiple_of(j * tk, tk)
        out = []
        for mp in range(2):
            s = s_in[mp]
            vv = vt_ref[0, 0, v_rows[mp], pl.ds(off, tk)]
            if bounds is None:
                m_new = jnp.maximum(ms[mp], jnp.max(s, axis=0, keepdims=True))
                alpha = jnp.exp2((ms[mp] - m_new) * c)
                p = jnp.exp2((s - m_new) * c).astype(BF16)
                acc_refs[mp][...] = alpha * acc_refs[mp][...] + jnp.dot(vv, p, preferred_element_type=F32)
            else:
                m_new = ms[mp]
                p = jnp.exp2(s * c - bounds[mp]).astype(BF16)
                acc_refs[mp][...] += jnp.dot(vv, p, preferred_element_type=F32)
            out.append(m_new)
        return tuple(out)

    def pair(i, ms):
        j = 2 * i
        scores(j + 1, sb_ref)
        ms = process(j, sa_ref, ms)
        scores(j + 2, sa_ref)
        return process(j + 1, sb_ref, ms)

    m0 = jnp.full((1, tq), NEG_INIT, F32)
    ms = (m0, m0)
    scores(0, sa_ref)
    if nk % 2 == 1:
        ms = lax.fori_loop(0, (nk - 1) // 2, pair, ms)
        process(nk - 1, sa_ref, ms)
    else:
        ms = lax.fori_loop(0, nk // 2 - 1, pair, ms)
        scores(nk - 1, sb_ref)
        ms = process(nk - 2, sa_ref, ms)
        process(nk - 1, sb_ref, ms)


def _da_attn_kernel(lam_ref, gain_ref, qt_ref, k_ref, vt_ref, *rest, nk, tk, c, out_scale, bounded=False):
    o_ref, acc1_ref, acc2_ref, sa_ref, sb_ref = rest[-5:]
    bounds = (rest[0][0, 0, 0:1, :], rest[0][0, 0, 1:2, :]) if bounded else None
    qt = qt_ref[0]
    sub = lax.broadcasted_iota(jnp.int32, (LANES, 1), 0)
    zero = jnp.zeros_like(qt)
    q1 = jnp.where(sub < DA_HEAD_DIM, qt, zero)
    q2 = jnp.where(sub >= DA_HEAD_DIM, qt, zero)
    full = slice(None)
    _attn_t_pipeline((q1, q2), (full, full), (full, full), k_ref, vt_ref, (acc1_ref, acc2_ref), sa_ref, sb_ref,
                     nk=nk, tk=tk, c=c, bounds=bounds)
    a1 = acc1_ref[...]
    a2 = acc2_ref[...]
    dv = LANES
    o = a1[:dv] / a1[dv:dv + 1] - lam_ref[0] * (a2[:dv] / a2[dv:dv + 1])
    o = o.T
    y = o * lax.rsqrt(jnp.mean(o * o, axis=-1, keepdims=True) + EPS)
    o_ref[0] = (y * gain_ref[...] * out_scale).astype(o_ref.dtype)


def _mla_attn_kernel(qt_ref, k_ref, vt_ref, *rest, nk, tk, c, bounded=False):
    o_ref, acc1_ref, acc2_ref, sa_ref, sb_ref = rest[-5:]
    bounds = (rest[0][0, 0, 0:1, :], rest[0][0, 0, 1:2, :]) if bounded else None
    qa = qt_ref[0, :LANES, :]
    qb = qt_ref[0, LANES:, :]
    vr = MLA_V + ONES_ROWS
    _attn_t_pipeline((qa, qb), (slice(0, LANES), slice(LANES, 2 * LANES)), (slice(0, vr), slice(vr, 2 * vr)),
                     k_ref, vt_ref, (acc1_ref, acc2_ref), sa_ref, sb_ref, nk=nk, tk=tk, c=c, bounds=bounds)
    a1 = acc1_ref[...]
    a2 = acc2_ref[...]
    o = jnp.concatenate([a1[:MLA_V] / a1[MLA_V:MLA_V + 1], a2[:MLA_V] / a2[MLA_V:MLA_V + 1]], axis=0)
    o_ref[0] = o.T.astype(o_ref.dtype)


def _pick_tk(t):
    return 1408 if t % 1408 == 0 else 256


def _attn_call(kernel, qt, k, vt, extra, *, acc_rows, q0, nq, tq, k0, nk, tk, name, mb=None):
    b, _, t = qt.shape
    g = vt.shape[1]
    qr = qt.shape[1] // g
    kl = k.shape[2] // g
    vr = vt.shape[2]
    body = functools.partial(kernel, nk=nk, tk=tk, bounded=mb is not None)
    mb_specs = [] if mb is None else [pl.BlockSpec((1, 1, 8, tq), lambda bi, gi, qi: (bi, gi, 0, q0 + qi))]
    mb_args = [] if mb is None else [mb]
    extra_specs = [pl.BlockSpec(memory_space=pltpu.SMEM) if e.ndim == 1 else
                   pl.BlockSpec(e.shape, lambda bi, gi, qi: (0, 0)) for e in extra]
    return pl.pallas_call(
        body,
        grid=(b, g, nq),
        in_specs=extra_specs + [
            pl.BlockSpec((1, qr, tq), lambda bi, gi, qi: (bi, gi, q0 + qi)),
            pl.BlockSpec((1, nk * tk, kl), lambda bi, gi, qi: (bi, k0, gi)),
            pl.BlockSpec((1, 1, vr, nk * tk), lambda bi, gi, qi: (bi, gi, 0, k0)),
        ] + mb_specs,
        out_specs=pl.BlockSpec((1, tq, LANES), lambda bi, gi, qi: (bi, qi, gi)),
        out_shape=jax.ShapeDtypeStruct((b, nq * tq, g * LANES), BF16),
        scratch_shapes=[pltpu.VMEM((acc_rows, tq), F32), pltpu.VMEM((acc_rows, tq), F32),
                        pltpu.VMEM((2, tk, tq), F32), pltpu.VMEM((2, tk, tq), F32)],
        compiler_params=pltpu.CompilerParams(
            dimension_semantics=("parallel", "parallel", "arbitrary"), vmem_limit_bytes=VMEM_LIMIT),
        name=name,
    )(*extra, qt, k, vt, *mb_args)


def _attention(kernel, q, k, v, extra, n_lat, dv, name, c):
    b, t, _ = q.shape
    n_ctx = t - n_lat
    qt = jnp.swapaxes(q, 1, 2)
    h = v.shape[2] // dv
    vt = jnp.swapaxes(v, 1, 2).reshape(b, h, dv, t)
    vt = jnp.concatenate([vt, jnp.ones((b, h, ONES_ROWS, t), vt.dtype)], axis=2)
    g = v.shape[2] // LANES
    vt = vt.reshape(b, g, -1, t)
    tk = _pick_tk(t)
    tq = min(ATTN_TQ, n_lat)
    acc_rows = dv + ONES_ROWS
    lat = dict(acc_rows=acc_rows, q0=0, nq=n_lat // tq, tq=tq, k0=0, nk=t // tk, tk=tk, name=name + "_lat")
    qf = q.astype(F32).reshape(b, t, g, 2, -1)
    kf = k.astype(F32).reshape(b, t, g, 2, -1)
    qn = jnp.sqrt(jnp.sum(qf * qf, axis=-1))
    kmax = jnp.max(jnp.sqrt(jnp.sum(kf * kf, axis=-1)), axis=1, keepdims=True)
    mb = (c * qn * kmax).transpose(0, 2, 3, 1)
    mb = jnp.pad(mb, ((0, 0), (0, 0), (0, 6), (0, 0)))
    o_lat = lax.cond(jnp.max(mb) < BOUND_SAFE,
                     lambda: _attn_call(kernel, qt, k, vt, extra, mb=mb, **lat),
                     lambda: _attn_call(kernel, qt, k, vt, extra, **lat))
    o_ctx = _attn_call(kernel, qt, k, vt, extra, acc_rows=acc_rows, q0=n_lat // n_ctx, nq=1, tq=n_ctx,
                       k0=n_lat // n_ctx, nk=1, tk=n_ctx, name=name + "_ctx")
    return jnp.concatenate([o_lat, o_ctx], axis=1)


def _rms_norm(x, g):
    xf = x.astype(F32)
    y = xf * lax.rsqrt(jnp.mean(xf * xf, axis=-1, keepdims=True) + EPS)
    return y * g.astype(F32)


def _l2_norm(x):
    return x * lax.rsqrt(jnp.sum(x * x, axis=-1, keepdims=True) + EPS)


def _rope_tables(rows, dim):
    nf = dim // 4
    inv = ROPE_BASE ** (-jnp.arange(nf, dtype=F32) / nf)
    r = jnp.repeat(jnp.arange(rows, dtype=F32), GRID_W)
    c = jnp.tile(jnp.arange(GRID_W, dtype=F32), rows)
    ang = jnp.concatenate([r[:, None] * inv, c[:, None] * inv], axis=-1)
    return jnp.cos(ang), jnp.sin(ang)


def _apply_rope(x, cos, sin):
    half = x.shape[-1] // 2
    x1, x2 = x[..., :half], x[..., half:]
    cs, sn = cos[:, None, :], sin[:, None, :]
    return jnp.concatenate([x1 * cs - x2 * sn, x2 * cs + x1 * sn], axis=-1)


def _diff_attention(hc, hl, w_in, q_gain, k_gain, lam_vecs, sub_gain, w_out, lam_init, cos, sin):
    lv = lam_vecs.astype(F32)
    lam = jnp.exp(jnp.sum(lv[0] * lv[1])) - jnp.exp(jnp.sum(lv[2] * lv[3])) + lam_init
    c = DA_HEAD_DIM ** -0.5 * LOG2E

    def project(h, rope):
        b, n, _ = h.shape
        q, k, v = jnp.split(h @ w_in, 3, axis=-1)
        q = _rms_norm(q.reshape(b, n, 2 * DA_HEADS, DA_HEAD_DIM), q_gain)
        k = _rms_norm(k.reshape(b, n, 2 * DA_HEADS, DA_HEAD_DIM), k_gain)
        if rope:
            q = _apply_rope(q, cos, sin)
            k = _apply_rope(k, cos, sin)
        return q.reshape(b, n, -1), k.reshape(b, n, -1), v

    ql, kl, vl = project(hl, True)
    qc, kc, vc = project(hc, False)
    q = jnp.concatenate([ql, qc], axis=1).astype(BF16)
    k = jnp.concatenate([kl, kc], axis=1).astype(BF16)
    v = jnp.concatenate([vl, vc], axis=1).astype(BF16)
    n = hl.shape[1]
    kern = functools.partial(_da_attn_kernel, c=c, out_scale=1.0 - lam_init)
    o = _attention(kern, q, k, v, [lam.reshape(1), sub_gain.reshape(1, -1).astype(F32)], n, 2 * DA_HEAD_DIM, "da", c)
    out = o.astype(F32) @ w_out
    return out[:, n:], out[:, :n]


def _mla_attention(hc, hl, w_down, q_a_gain, kv_a_gain, w_uq, w_ukv, q_gain, k_gain, w_out, cos, sin):
    c = MLA_QK ** -0.5 * LOG2E
    pad = LANES - MLA_QK

    def project(h, rope):
        b, n, _ = h.shape
        lat = h @ w_down
        cq = lat[..., :MLA_Q_RANK]
        ckv = lat[..., MLA_Q_RANK:MLA_Q_RANK + MLA_KV_RANK]
        kr = lat[..., MLA_Q_RANK + MLA_KV_RANK:]
        q = (_rms_norm(cq, q_a_gain) @ w_uq).reshape(b, n, MLA_HEADS, MLA_QK)
        kv = (_rms_norm(ckv, kv_a_gain) @ w_ukv).reshape(b, n, MLA_HEADS, MLA_NOPE + MLA_V)
        k = jnp.concatenate([kv[..., :MLA_NOPE],
                             jnp.broadcast_to(kr[:, :, None, :], (b, n, MLA_HEADS, MLA_ROPE))], axis=-1)
        v = kv[..., MLA_NOPE:]
        q = _rms_norm(q, q_gain)
        k = _rms_norm(k, k_gain)
        if rope:
            q = jnp.concatenate([q[..., :MLA_NOPE], _apply_rope(q[..., MLA_NOPE:], cos, sin)], axis=-1)
            k = jnp.concatenate([k[..., :MLA_NOPE], _apply_rope(k[..., MLA_NOPE:], cos, sin)], axis=-1)
        q = jnp.pad(q, ((0, 0), (0, 0), (0, 0), (0, pad)))
        k = jnp.pad(k, ((0, 0), (0, 0), (0, 0), (0, pad)))
        return q.reshape(b, n, -1), k.reshape(b, n, -1), v.reshape(b, n, -1)

    ql, kl, vl = project(hl, True)
    qc, kc, vc = project(hc, False)
    q = jnp.concatenate([ql, qc], axis=1).astype(BF16)
    k = jnp.concatenate([kl, kc], axis=1).astype(BF16)
    v = jnp.concatenate([vl, vc], axis=1).astype(BF16)
    n = hl.shape[1]
    o = _attention(functools.partial(_mla_attn_kernel, c=c), q, k, v, [], n, MLA_V, "mla", c)
    out = o.astype(F32) @ w_out
    return out[:, n:], out[:, :n]


def _centred_depthwise_conv(x, w):
    k = w.shape[0]
    return lax.conv_general_dilated(x, w[:, None, :].astype(x.dtype), window_strides=(1,),
                                    padding=[((k - 1) // 2, k // 2)],
                                    dimension_numbers=('NWC', 'WIO', 'NWC'),
                                    feature_group_count=x.shape[-1])


def _gated_delta_chunked(q, k, v, g, beta, s0):
    b, n, h = q.shape[:3]
    c = GDN_CHUNK

    def to_chunks(t):
        t = t.astype(F32).reshape((b, n // c, c, h) + t.shape[3:])
        return t.transpose((1, 0, 3, 2) + tuple(range(4, t.ndim)))

    q, k, v, g, beta = to_chunks(q), to_chunks(k), to_chunks(v), to_chunks(g), to_chunks(beta)
    decay = jnp.cumsum(g, axis=-1)
    tril = jnp.tril(jnp.ones((c, c), bool))
    strict = jnp.tril(jnp.ones((c, c), bool), -1)
    diff = decay[..., :, None] - decay[..., None, :]
    gamma = jnp.where(tril, jnp.exp(jnp.where(tril, diff, 0.0)), 0.0)
    kb = k * beta[..., None]
    vb = v * beta[..., None]
    a_mat = jnp.where(strict, jnp.einsum('...id,...jd->...ij', kb, k) * gamma, 0.0)
    eye = jnp.eye(c, dtype=F32)
    rhs = jnp.concatenate([vb, kb * jnp.exp(decay)[..., None]], axis=-1)
    sol = lax.linalg.triangular_solve(a_mat + eye, rhs, left_side=True, lower=True, unit_diagonal=True)
    dv = v.shape[-1]
    u, w = sol[..., :dv], sol[..., dv:]
    qk = jnp.where(tril, jnp.einsum('...id,...jd->...ij', q, k) * gamma, 0.0)
    q_dec = q * jnp.exp(decay)[..., None]
    k_dec = k * jnp.exp(decay[..., -1:] - decay)[..., None]
    last = jnp.exp(decay[..., -1])

    def step(s, inp):
        qd, kd, ww, uu, qkc, lst = inp
        v_new = uu - jnp.einsum('bhck,bhkv->bhcv', ww, s)
        o = jnp.einsum('bhck,bhkv->bhcv', qd, s) + jnp.einsum('bhcj,bhjv->bhcv', qkc, v_new)
        s = s * lst[..., None, None] + jnp.einsum('bhck,bhcv->bhkv', kd, v_new)
        return s, o

    s_fin, o = lax.scan(step, s0.astype(F32), (q_dec, k_dec, w, u, qk, last))
    o = o.transpose(1, 0, 3, 2, 4).reshape(b, n, h, dv)
    return o, s_fin


def _gated_deltanet(hc, hl, w_in, conv_w, a_log, dt_bias, o_gain, w_out, need_ctx):
    nqk = GDN_HEADS * GDN_DK
    nv = GDN_HEADS * GDN_DV
    nqkv = 2 * nqk + nv

    def prep(h):
        b, n, _ = h.shape
        p = h @ w_in
        qkv = jax.nn.silu(_centred_depthwise_conv(p[..., :nqkv], conv_w))
        q = _l2_norm(qkv[..., :nqk].reshape(b, n, GDN_HEADS, GDN_DK)) * (GDN_DK ** -0.5)
        k = _l2_norm(qkv[..., nqk:2 * nqk].reshape(b, n, GDN_HEADS, GDN_DK))
        v = qkv[..., 2 * nqk:].reshape(b, n, GDN_HEADS, GDN_DV)
        z = p[..., nqkv:nqkv + nv].reshape(b, n, GDN_HEADS, GDN_DV)
        ab = p[..., nqkv + nv:].astype(F32).reshape(b, n, 2, 2, GDN_HEADS)
        g = -jnp.exp(a_log.astype(F32)) * jax.nn.softplus(ab[:, :, 0] + dt_bias.astype(F32))
        beta = jax.nn.sigmoid(ab[:, :, 1])
        return (q, k, v, g, beta), z

    def scan_dir(t, d, s0):
        q, k, v, g, beta = t
        g, beta = g[:, :, d], beta[:, :, d]
        if d == 1:
            q, k, v, g, beta = [jnp.flip(a, axis=1) for a in (q, k, v, g, beta)]
        o, s = _gated_delta_chunked(q, k, v, g, beta, s0)
        if d == 1:
            o = jnp.flip(o, axis=1)
        return o, s

    tc, zc = prep(hc)
    tl, zl = prep(hl)
    s0 = jnp.zeros((hl.shape[0], GDN_HEADS, GDN_DK, GDN_DV), F32)
    oc_f, sc_f = scan_dir(tc, 0, s0)
    ol_f, _ = scan_dir(tl, 0, sc_f)
    oc_b, sc_b = scan_dir(tc, 1, s0)
    ol_b, _ = scan_dir(tl, 1, sc_b)

    def finish(o, z):
        b, n = o.shape[:2]
        o = _rms_norm(o, o_gain) * jax.nn.silu(z)
        return o.reshape(b, n, -1) @ w_out

    out_l = finish(ol_f + ol_b, zl)
    out_c = finish(oc_f + oc_b, zc) if need_ctx else None
    return out_c, out_l


def _expert_choice_ffn(h, w_router, w_gate, w_up, w_down):
    b, n, _ = h.shape
    cap = EC_FACTOR * n // N_EXPERTS
    aff = jax.nn.softmax((h @ w_router).astype(F32), axis=-1)
    gate, idx = lax.top_k(aff.transpose(0, 2, 1), cap)
    bi = jnp.arange(b)[:, None, None]
    xs = h[bi, idx]
    hg = jnp.einsum('becd,edf->becf', xs, w_gate)
    hu = jnp.einsum('becd,edf->becf', xs, w_up)
    y = jnp.einsum('becf,efd->becd', jax.nn.silu(hg) * hu, w_down) * gate[..., None].astype(h.dtype)
    return jnp.zeros_like(h).at[bi, idx].add(y)


def kernel(x, c, ctx, c_ctx, ada_w, ada_b, norm_g, da_w_in, da_q_gain, da_k_gain, da_lambda, da_sub_gain, da_w_out, mla_w_down, mla_q_a_gain, mla_kv_a_gain, mla_w_uq, mla_w_ukv, mla_q_gain, mla_k_gain, mla_w_out, gdn_w_in, gdn_conv_w, gdn_a_log, gdn_dt_bias, gdn_o_gain, gdn_w_out, moe_router, moe_w_gate, moe_w_up, moe_w_down):
    depth = ada_w.shape[0]
    rows = x.shape[1] // GRID_W
    da_cos, da_sin = _rope_tables(rows, DA_HEAD_DIM)
    mla_cos, mla_sin = _rope_tables(rows, MLA_ROPE)
    cs = ctx
    sc = jax.nn.silu(c)
    scc = jax.nn.silu(c_ctx)
    for l in range(depth):
        last = l == depth - 1
        j = l // N_MIXERS
        kind = l % N_MIXERS
        mod_l = (sc @ ada_w[l] + ada_b[l])[:, None, :]
        mod_c = scc @ ada_w[l] + ada_b[l]
        sh1l, sc1l, g1l, sh2l, sc2l, g2l = jnp.split(mod_l, 6, axis=-1)
        sh1c, sc1c, g1c, sh2c, sc2c, g2c = jnp.split(mod_c, 6, axis=-1)
        hl = _rms_norm(x, norm_g[l, 0]) * (1.0 + sc1l) + sh1l
        hc = _rms_norm(cs, norm_g[l, 0]) * (1.0 + sc1c) + sh1c
        if kind == 0:
            lam_init = 0.8 - 0.6 * math.exp(-0.3 * l)
            mc, ml = _diff_attention(hc, hl, da_w_in[j], da_q_gain[j], da_k_gain[j], da_lambda[j],
                                     da_sub_gain[j], da_w_out[j], lam_init, da_cos, da_sin)
        elif kind == 1:
            mc, ml = _mla_attention(hc, hl, mla_w_down[j], mla_q_a_gain[j], mla_kv_a_gain[j], mla_w_uq[j],
                                    mla_w_ukv[j], mla_q_gain[j], mla_k_gain[j], mla_w_out[j], mla_cos, mla_sin)
        else:
            mc, ml = _gated_deltanet(hc, hl, gdn_w_in[j], gdn_conv_w[j], gdn_a_log[j], gdn_dt_bias[j],
                                     gdn_o_gain[j], gdn_w_out[j], not last)
        x = x + g1l * ml
        hl2 = _rms_norm(x, norm_g[l, 1]) * (1.0 + sc2l) + sh2l
        x = x + g2l * _expert_choice_ffn(hl2, moe_router[l], moe_w_gate[l], moe_w_up[l], moe_w_down[l])
        if not last:
            cs = cs + g1c * mc
            hc2 = _rms_norm(cs, norm_g[l, 1]) * (1.0 + sc2c) + sh2c
            cs = cs + g2c * _expert_choice_ffn(hc2, moe_router[l], moe_w_gate[l], moe_w_up[l], moe_w_down[l])
    return x
```

```python
import functools
import math

import jax
import jax.numpy as jnp
from jax import lax
from jax.experimental import pallas as pl
from jax.experimental.pallas import tpu as pltpu

F32 = jnp.float32
BF16 = jnp.bfloat16

GRID_W = 64
N_MIXERS = 3
ROPE_BASE = 10000.0
EPS = 1e-6

DA_HEADS = 8
DA_HEAD_DIM = 64
MLA_HEADS = 16
MLA_Q_RANK = 256
MLA_KV_RANK = 128
MLA_NOPE = 64
MLA_ROPE = 32
MLA_V = 64
MLA_QK = MLA_NOPE + MLA_ROPE
GDN_HEADS = 8
GDN_DK = 128
GDN_DV = 128
GDN_CHUNK = 64
N_EXPERTS = 16
EC_FACTOR = 2

LANES = 128
ATTN_TQ = 1024
LOG2E = math.log2(math.e)
NEG_INIT = -1e30
BOUND_SAFE = 60.0
VMEM_LIMIT = 56 * 1024 * 1024


ONES_ROWS = 8


def _attn_t_pipeline(q_maps, k_lanes, v_rows, k_ref, vt_ref, acc_refs, sa_ref, sb_ref, *, nk, tk, c, bounds=None):
    tq = q_maps[0].shape[1]
    for acc in acc_refs:
        acc[...] = jnp.zeros_like(acc)

    def scores(j, s_out):
        off = pl.multiple_of(j * tk, tk)
        for mp in range(2):
            kk = k_ref[0, pl.ds(off, tk), k_lanes[mp]]
            s_out[mp] = jnp.dot(kk, q_maps[mp], preferred_element_type=F32)

    def process(j, s_in, ms):
        off = pl.multiple_of(j * tk, tk)
        out = []
        for mp in range(2):
            s = s_in[mp]
            vv = vt_ref[0, 0, v_rows[mp], pl.ds(off, tk)]
            if bounds is None:
                m_new = jnp.maximum(ms[mp], jnp.max(s, axis=0, keepdims=True))
                alpha = jnp.exp2((ms[mp] - m_new) * c)
                p = jnp.exp2((s - m_new) * c).astype(BF16)
                acc_refs[mp][...] = alpha * acc_refs[mp][...] + jnp.dot(vv, p, preferred_element_type=F32)
            else:
                m_new = ms[mp]
                p = jnp.exp2(s * c - bounds[mp]).astype(BF16)
                acc_refs[mp][...] += jnp.dot(vv, p, preferred_element_type=F32)
            out.append(m_new)
        return tuple(out)

    def pair(i, ms):
        j = 2 * i
        scores(j + 1, sb_ref)
        ms = process(j, sa_ref, ms)
        scores(j + 2, sa_ref)
        return process(j + 1, sb_ref, ms)

    m0 = jnp.full((1, tq), NEG_INIT, F32)
    ms = (m0, m0)
    scores(0, sa_ref)
    if nk % 2 == 1:
        ms = lax.fori_loop(0, (nk - 1) // 2, pair, ms)
        process(nk - 1, sa_ref, ms)
    else:
        ms = lax.fori_loop(0, nk // 2 - 1, pair, ms)
        scores(nk - 1, sb_ref)
        ms = process(nk - 2, sa_ref, ms)
        process(nk - 1, sb_ref, ms)


def _da_attn_kernel(lam_ref, gain_ref, qt_ref, k_ref, vt_ref, *rest, nk, tk, c, out_scale, bounded=False):
    o_ref, acc1_ref, acc2_ref, sa_ref, sb_ref = rest[-5:]
    bounds = (rest[0][0, 0, 0:1, :], rest[0][0, 0, 1:2, :]) if bounded else None
    qt = qt_ref[0]
    sub = lax.broadcasted_iota(jnp.int32, (LANES, 1), 0)
    zero = jnp.zeros_like(qt)
    q1 = jnp.where(sub < DA_HEAD_DIM, qt, zero)
    q2 = jnp.where(sub >= DA_HEAD_DIM, qt, zero)
    full = slice(None)
    _attn_t_pipeline((q1, q2), (full, full), (full, full), k_ref, vt_ref, (acc1_ref, acc2_ref), sa_ref, sb_ref,
                     nk=nk, tk=tk, c=c, bounds=bounds)
    a1 = acc1_ref[...]
    a2 = acc2_ref[...]
    dv = LANES
    o = a1[:dv] / a1[dv:dv + 1] - lam_ref[0] * (a2[:dv] / a2[dv:dv + 1])
    o = o.T
    y = o * lax.rsqrt(jnp.mean(o * o, axis=-1, keepdims=True) + EPS)
    o_ref[0] = (y * gain_ref[...] * out_scale).astype(o_ref.dtype)


def _mla_attn_kernel(qt_ref, k_ref, vt_ref, *rest, nk, tk, c, bounded=False):
    o_ref, acc1_ref, acc2_ref, sa_ref, sb_ref = rest[-5:]
    bounds = (rest[0][0, 0, 0:1, :], rest[0][0, 0, 1:2, :]) if bounded else None
    qa = qt_ref[0, :LANES, :]
    qb = qt_ref[0, LANES:, :]
    vr = MLA_V + ONES_ROWS
    _attn_t_pipeline((qa, qb), (slice(0, LANES), slice(LANES, 2 * LANES)), (slice(0, vr), slice(vr, 2 * vr)),
                     k_ref, vt_ref, (acc1_ref, acc2_ref), sa_ref, sb_ref, nk=nk, tk=tk, c=c, bounds=bounds)
    a1 = acc1_ref[...]
    a2 = acc2_ref[...]
    o = jnp.concatenate([a1[:MLA_V] / a1[MLA_V:MLA_V + 1], a2[:MLA_V] / a2[MLA_V:MLA_V + 1]], axis=0)
    o_ref[0] = o.T.astype(o_ref.dtype)


def _pick_tk(t):
    return 1408 if t % 1408 == 0 else 256


def _attn_call(kernel, qt, k, vt, extra, *, acc_rows, q0, nq, tq, k0, nk, tk, name, mb=None):
    b, _, t = qt.shape
    g = vt.shape[1]
    qr = qt.shape[1] // g
    kl = k.shape[2] // g
    vr = vt.shape[2]
    body = functools.partial(kernel, nk=nk, tk=tk, bounded=mb is not None)
    mb_specs = [] if mb is None else [pl.BlockSpec((1, 1, 8, tq), lambda bi, gi, qi: (bi, gi, 0, q0 + qi))]
    mb_args = [] if mb is None else [mb]
    extra_specs = [pl.BlockSpec(memory_space=pltpu.SMEM) if e.ndim == 1 else
                   pl.BlockSpec(e.shape, lambda bi, gi, qi: (0, 0)) for e in extra]
    return pl.pallas_call(
        body,
        grid=(b, g, nq),
        in_specs=extra_specs + [
            pl.BlockSpec((1, qr, tq), lambda bi, gi, qi: (bi, gi, q0 + qi)),
            pl.BlockSpec((1, nk * tk, kl), lambda bi, gi, qi: (bi, k0, gi)),
            pl.BlockSpec((1, 1, vr, nk * tk), lambda bi, gi, qi: (bi, gi, 0, k0)),
        ] + mb_specs,
        out_specs=pl.BlockSpec((1, tq, LANES), lambda bi, gi, qi: (bi, qi, gi)),
        out_shape=jax.ShapeDtypeStruct((b, nq * tq, g * LANES), BF16),
        scratch_shapes=[pltpu.VMEM((acc_rows, tq), F32), pltpu.VMEM((acc_rows, tq), F32),
                        pltpu.VMEM((2, tk, tq), F32), pltpu.VMEM((2, tk, tq), F32)],
        compiler_params=pltpu.CompilerParams(
            dimension_semantics=("parallel", "parallel", "arbitrary"), vmem_limit_bytes=VMEM_LIMIT),
        name=name,
    )(*extra, qt, k, vt, *mb_args)


def _attention(kernel, q, k, v, extra, n_lat, dv, name, c):
    b, t, _ = q.shape
    n_ctx = t - n_lat
    qt = jnp.swapaxes(q, 1, 2)
    h = v.shape[2] // dv
    vt = jnp.swapaxes(v, 1, 2).reshape(b, h, dv, t)
    vt = jnp.concatenate([vt, jnp.ones((b, h, ONES_ROWS, t), vt.dtype)], axis=2)
    g = v.shape[2] // LANES
    vt = vt.reshape(b, g, -1, t)
    tk = _pick_tk(t)
    tq = min(ATTN_TQ, n_lat)
    acc_rows = dv + ONES_ROWS
    lat = dict(acc_rows=acc_rows, q0=0, nq=n_lat // tq, tq=tq, k0=0, nk=t // tk, tk=tk, name=name + "_lat")
    qf = q.astype(F32).reshape(b, t, g, 2, -1)
    kf = k.astype(F32).reshape(b, t, g, 2, -1)
    qn = jnp.sqrt(jnp.sum(qf * qf, axis=-1))
    kmax = jnp.max(jnp.sqrt(jnp.sum(kf * kf, axis=-1)), axis=1, keepdims=True)
    mb = (c * qn * kmax).transpose(0, 2, 3, 1)
    mb = jnp.pad(mb, ((0, 0), (0, 0), (0, 6), (0, 0)))
    o_lat = lax.cond(jnp.max(mb) < BOUND_SAFE,
                     lambda: _attn_call(kernel, qt, k, vt, extra, mb=mb, **lat),
                     lambda: _attn_call(kernel, qt, k, vt, extra, **lat))
    o_ctx = _attn_call(kernel, qt, k, vt, extra, acc_rows=acc_rows, q0=n_lat // n_ctx, nq=1, tq=n_ctx,
                       k0=n_lat // n_ctx, nk=1, tk=n_ctx, name=name + "_ctx")
    return jnp.concatenate([o_lat, o_ctx], axis=1)


def _rms_norm(x, g):
    xf = x.astype(F32)
    y = xf * lax.rsqrt(jnp.mean(xf * xf, axis=-1, keepdims=True) + EPS)
    return y * g.astype(F32)


def _l2_norm(x):
    return x * lax.rsqrt(jnp.sum(x * x, axis=-1, keepdims=True) + EPS)


def _rope_tables(rows, dim):
    nf = dim // 4
    inv = ROPE_BASE ** (-jnp.arange(nf, dtype=F32) / nf)
    r = jnp.repeat(jnp.arange(rows, dtype=F32), GRID_W)
    c = jnp.tile(jnp.arange(GRID_W, dtype=F32), rows)
    ang = jnp.concatenate([r[:, None] * inv, c[:, None] * inv], axis=-1)
    return jnp.cos(ang), jnp.sin(ang)


def _pad_rope(cos, sin, n_ctx):
    return (jnp.concatenate([cos, jnp.ones((n_ctx, cos.shape[1]), cos.dtype)], axis=0),
            jnp.concatenate([sin, jnp.zeros((n_ctx, sin.shape[1]), sin.dtype)], axis=0))


def _apply_rope(x, cos, sin):
    half = x.shape[-1] // 2
    x1, x2 = x[..., :half], x[..., half:]
    cs, sn = cos[:, None, :], sin[:, None, :]
    return jnp.concatenate([x1 * cs - x2 * sn, x2 * cs + x1 * sn], axis=-1)


def _diff_attention(hc, hl, w_in, q_gain, k_gain, lam_vecs, sub_gain, w_out, lam_init, cos, sin):
    lv = lam_vecs.astype(F32)
    lam = jnp.exp(jnp.sum(lv[0] * lv[1])) - jnp.exp(jnp.sum(lv[2] * lv[3])) + lam_init
    c = DA_HEAD_DIM ** -0.5 * LOG2E

    n = hl.shape[1]
    h = jnp.concatenate([hl, hc], axis=1)
    b, t, _ = h.shape
    cos, sin = _pad_rope(cos, sin, t - n)
    q, k, v = jnp.split(h @ w_in, 3, axis=-1)
    q = _apply_rope(_rms_norm(q.reshape(b, t, 2 * DA_HEADS, DA_HEAD_DIM), q_gain), cos, sin)
    k = _apply_rope(_rms_norm(k.reshape(b, t, 2 * DA_HEADS, DA_HEAD_DIM), k_gain), cos, sin)
    q = q.reshape(b, t, -1).astype(BF16)
    k = k.reshape(b, t, -1).astype(BF16)
    v = v.astype(BF16)
    kern = functools.partial(_da_attn_kernel, c=c, out_scale=1.0 - lam_init)
    o = _attention(kern, q, k, v, [lam.reshape(1), sub_gain.reshape(1, -1).astype(F32)], n, 2 * DA_HEAD_DIM, "da", c)
    out = o.astype(F32) @ w_out
    return out[:, n:], out[:, :n]


def _mla_attention(hc, hl, w_down, q_a_gain, kv_a_gain, w_uq, w_ukv, q_gain, k_gain, w_out, cos, sin):
    c = MLA_QK ** -0.5 * LOG2E
    pad = LANES - MLA_QK

    n = hl.shape[1]
    h = jnp.concatenate([hl, hc], axis=1)
    cos, sin = _pad_rope(cos, sin, h.shape[1] - n)

    def project(h):
        b, n, _ = h.shape
        lat = h @ w_down
        cq = lat[..., :MLA_Q_RANK]
        ckv = lat[..., MLA_Q_RANK:MLA_Q_RANK + MLA_KV_RANK]
        kr = lat[..., MLA_Q_RANK + MLA_KV_RANK:]
        q = (_rms_norm(cq, q_a_gain) @ w_uq).reshape(b, n, MLA_HEADS, MLA_QK)
        kv = (_rms_norm(ckv, kv_a_gain) @ w_ukv).reshape(b, n, MLA_HEADS, MLA_NOPE + MLA_V)
        k = jnp.concatenate([kv[..., :MLA_NOPE],
                             jnp.broadcast_to(kr[:, :, None, :], (b, n, MLA_HEADS, MLA_ROPE))], axis=-1)
        v = kv[..., MLA_NOPE:]
        q = _rms_norm(q, q_gain)
        k = _rms_norm(k, k_gain)
        q = jnp.concatenate([q[..., :MLA_NOPE], _apply_rope(q[..., MLA_NOPE:], cos, sin)], axis=-1)
        k = jnp.concatenate([k[..., :MLA_NOPE], _apply_rope(k[..., MLA_NOPE:], cos, sin)], axis=-1)
        q = jnp.pad(q, ((0, 0), (0, 0), (0, 0), (0, pad)))
        k = jnp.pad(k, ((0, 0), (0, 0), (0, 0), (0, pad)))
        return q.reshape(b, n, -1), k.reshape(b, n, -1), v.reshape(b, n, -1)

    q, k, v = [a.astype(BF16) for a in project(h)]
    o = _attention(functools.partial(_mla_attn_kernel, c=c), q, k, v, [], n, MLA_V, "mla", c)
    out = o.astype(F32) @ w_out
    return out[:, n:], out[:, :n]


def _centred_depthwise_conv(x, w):
    k = w.shape[0]
    return lax.conv_general_dilated(x, w[:, None, :].astype(x.dtype), window_strides=(1,),
                                    padding=[((k - 1) // 2, k // 2)],
                                    dimension_numbers=('NWC', 'WIO', 'NWC'),
                                    feature_group_count=x.shape[-1])


def _gated_delta_chunked(q, k, v, g, beta, s0):
    b, n, h = q.shape[:3]
    c = GDN_CHUNK

    def to_chunks(t):
        t = t.astype(F32).reshape((b, n // c, c, h) + t.shape[3:])
        return t.transpose((1, 0, 3, 2) + tuple(range(4, t.ndim)))

    q, k, v, g, beta = to_chunks(q), to_chunks(k), to_chunks(v), to_chunks(g), to_chunks(beta)
    decay = jnp.cumsum(g, axis=-1)
    tril = jnp.tril(jnp.ones((c, c), bool))
    strict = jnp.tril(jnp.ones((c, c), bool), -1)
    diff = decay[..., :, None] - decay[..., None, :]
    gamma = jnp.where(tril, jnp.exp(jnp.where(tril, diff, 0.0)), 0.0)
    kb = k * beta[..., None]
    vb = v * beta[..., None]
    a_mat = jnp.where(strict, jnp.einsum('...id,...jd->...ij', kb, k) * gamma, 0.0)
    eye = jnp.eye(c, dtype=F32)
    rhs = jnp.concatenate([vb, kb * jnp.exp(decay)[..., None]], axis=-1)
    sol = lax.linalg.triangular_solve(a_mat + eye, rhs, left_side=True, lower=True, unit_diagonal=True)
    dv = v.shape[-1]
    u, w = sol[..., :dv], sol[..., dv:]
    qk = jnp.where(tril, jnp.einsum('...id,...jd->...ij', q, k) * gamma, 0.0)
    q_dec = q * jnp.exp(decay)[..., None]
    k_dec = k * jnp.exp(decay[..., -1:] - decay)[..., None]
    last = jnp.exp(decay[..., -1])

    def step(s, inp):
        qd, kd, ww, uu, qkc, lst = inp
        v_new = uu - jnp.einsum('bhck,bhkv->bhcv', ww, s)
        o = jnp.einsum('bhck,bhkv->bhcv', qd, s) + jnp.einsum('bhcj,bhjv->bhcv', qkc, v_new)
        s = s * lst[..., None, None] + jnp.einsum('bhck,bhcv->bhkv', kd, v_new)
        return s, o

    s_fin, o = lax.scan(step, s0.astype(F32), (q_dec, k_dec, w, u, qk, last))
    o = o.transpose(1, 0, 3, 2, 4).reshape(b, n, h, dv)
    return o, s_fin


def _gated_deltanet(hc, hl, w_in, conv_w, a_log, dt_bias, o_gain, w_out, need_ctx):
    nqk = GDN_HEADS * GDN_DK
    nv = GDN_HEADS * GDN_DV
    nqkv = 2 * nqk + nv

    def prep(h):
        b, n, _ = h.shape
        p = h @ w_in
        qkv = jax.nn.silu(_centred_depthwise_conv(p[..., :nqkv], conv_w))
        q = _l2_norm(qkv[..., :nqk].reshape(b, n, GDN_HEADS, GDN_DK)) * (GDN_DK ** -0.5)
        k = _l2_norm(qkv[..., nqk:2 * nqk].reshape(b, n, GDN_HEADS, GDN_DK))
        v = qkv[..., 2 * nqk:].reshape(b, n, GDN_HEADS, GDN_DV)
        z = p[..., nqkv:nqkv + nv].reshape(b, n, GDN_HEADS, GDN_DV)
        ab = p[..., nqkv + nv:].astype(F32).reshape(b, n, 2, 2, GDN_HEADS)
        g = -jnp.exp(a_log.astype(F32)) * jax.nn.softplus(ab[:, :, 0] + dt_bias.astype(F32))
        beta = jax.nn.sigmoid(ab[:, :, 1])
        return (q, k, v, g, beta), z

    def scan_dir(t, d, s0):
        q, k, v, g, beta = t
        g, beta = g[:, :, d], beta[:, :, d]
        if d == 1:
            q, k, v, g, beta = [jnp.flip(a, axis=1) for a in (q, k, v, g, beta)]
        o, s = _gated_delta_chunked(q, k, v, g, beta, s0)
        if d == 1:
            o = jnp.flip(o, axis=1)
        return o, s

    tc, zc = prep(hc)
    tl, zl = prep(hl)
    s0 = jnp.zeros((hl.shape[0], GDN_HEADS, GDN_DK, GDN_DV), F32)
    oc_f, sc_f = scan_dir(tc, 0, s0)
    ol_f, _ = scan_dir(tl, 0, sc_f)
    oc_b, sc_b = scan_dir(tc, 1, s0)
    ol_b, _ = scan_dir(tl, 1, sc_b)

    def finish(o, z):
        b, n = o.shape[:2]
        o = _rms_norm(o, o_gain) * jax.nn.silu(z)
        return o.reshape(b, n, -1) @ w_out

    out_l = finish(ol_f + ol_b, zl)
    out_c = finish(oc_f + oc_b, zc) if need_ctx else None
    return out_c, out_l


def _expert_choice_ffn(h, w_router, w_gate, w_up, w_down):
    b, n, _ = h.shape
    cap = EC_FACTOR * n // N_EXPERTS
    aff = jax.nn.softmax((h @ w_router).astype(F32), axis=-1)
    gate, idx = lax.top_k(aff.transpose(0, 2, 1), cap)
    bi = jnp.arange(b)[:, None, None]
    xs = h[bi, idx]
    hg = jnp.einsum('becd,edf->becf', xs, w_gate)
    hu = jnp.einsum('becd,edf->becf', xs, w_up)
    y = jnp.einsum('becf,efd->becd', jax.nn.silu(hg) * hu, w_down) * gate[..., None].astype(h.dtype)
    return jnp.zeros_like(h).at[bi, idx].add(y)


def kernel(x, c, ctx, c_ctx, ada_w, ada_b, norm_g, da_w_in, da_q_gain, da_k_gain, da_lambda, da_sub_gain, da_w_out, mla_w_down, mla_q_a_gain, mla_kv_a_gain, mla_w_uq, mla_w_ukv, mla_q_gain, mla_k_gain, mla_w_out, gdn_w_in, gdn_conv_w, gdn_a_log, gdn_dt_bias, gdn_o_gain, gdn_w_out, moe_router, moe_w_gate, moe_w_up, moe_w_down):
    depth = ada_w.shape[0]
    rows = x.shape[1] // GRID_W
    da_cos, da_sin = _rope_tables(rows, DA_HEAD_DIM)
    mla_cos, mla_sin = _rope_tables(rows, MLA_ROPE)
    cs = ctx
    sc = jax.nn.silu(c)
    scc = jax.nn.silu(c_ctx)
    for l in range(depth):
        last = l == depth - 1
        j = l // N_MIXERS
        kind = l % N_MIXERS
        mod_l = (sc @ ada_w[l] + ada_b[l])[:, None, :]
        mod_c = scc @ ada_w[l] + ada_b[l]
        sh1l, sc1l, g1l, sh2l, sc2l, g2l = jnp.split(mod_l, 6, axis=-1)
        sh1c, sc1c, g1c, sh2c, sc2c, g2c = jnp.split(mod_c, 6, axis=-1)
        hl = _rms_norm(x, norm_g[l, 0]) * (1.0 + sc1l) + sh1l
        hc = _rms_norm(cs, norm_g[l, 0]) * (1.0 + sc1c) + sh1c
        if kind == 0:
            lam_init = 0.8 - 0.6 * math.exp(-0.3 * l)
            mc, ml = _diff_attention(hc, hl, da_w_in[j], da_q_gain[j], da_k_gain[j], da_lambda[j],
                                     da_sub_gain[j], da_w_out[j], lam_init, da_cos, da_sin)
        elif kind == 1:
            mc, ml = _mla_attention(hc, hl, mla_w_down[j], mla_q_a_gain[j], mla_kv_a_gain[j], mla_w_uq[j],
                                    mla_w_ukv[j], mla_q_gain[j], mla_k_gain[j], mla_w_out[j], mla_cos, mla_sin)
        else:
            mc, ml = _gated_deltanet(hc, hl, gdn_w_in[j], gdn_conv_w[j], gdn_a_log[j], gdn_dt_bias[j],
                                     gdn_o_gain[j], gdn_w_out[j], not last)
        x = x + g1l * ml
        hl2 = _rms_norm(x, norm_g[l, 1]) * (1.0 + sc2l) + sh2l
        x = x + g2l * _expert_choice_ffn(hl2, moe_router[l], moe_w_gate[l], moe_w_up[l], moe_w_down[l])
        if not last:
            cs = cs + g1c * mc
            hc2 = _rms_norm(cs, norm_g[l, 1]) * (1.0 + sc2c) + sh2c
            cs = cs + g2c * _expert_choice_ffn(hc2, moe_router[l], moe_w_gate[l], moe_w_up[l], moe_w_down[l])
    return x
```

```python
import functools
import math

import jax
import jax.numpy as jnp
from jax import lax
from jax.experimental import pallas as pl
from jax.experimental.pallas import tpu as pltpu

F32 = jnp.float32
BF16 = jnp.bfloat16

GRID_W = 64
N_MIXERS = 3
ROPE_BASE = 10000.0
EPS = 1e-6

DA_HEADS = 8
DA_HEAD_DIM = 64
MLA_HEADS = 16
MLA_Q_RANK = 256
MLA_KV_RANK = 128
MLA_NOPE = 64
MLA_ROPE = 32
MLA_V = 64
MLA_QK = MLA_NOPE + MLA_ROPE
GDN_HEADS = 8
GDN_DK = 128
GDN_DV = 128
GDN_CHUNK = 64
N_EXPERTS = 16
EC_FACTOR = 2

LANES = 128
ATTN_TQ = 1024
LOG2E = math.log2(math.e)
NEG_INIT = -1e30
BOUND_SAFE = 60.0
VMEM_LIMIT = 56 * 1024 * 1024


ONES_ROWS = 8


def _attn_t_pipeline(q_maps, k_lanes, v_rows, k_ref, vt_ref, acc_refs, sa_ref, sb_ref, *, nk, tk, c, bounds=None,
                     data_rows=None):
    tq = q_maps[0].shape[1]
    for acc in acc_refs:
        acc[...] = jnp.zeros_like(acc)

    def scores(j, s_out):
        off = pl.multiple_of(j * tk, tk)
        for mp in range(2):
            kk = k_ref[0, pl.ds(off, tk), k_lanes[mp]]
            s_out[mp] = jnp.dot(kk, q_maps[mp], preferred_element_type=F32)

    def process(j, s_in, ms):
        off = pl.multiple_of(j * tk, tk)
        out = []
        for mp in range(2):
            s = s_in[mp]
            if bounds is None:
                vv = vt_ref[0, 0, v_rows[mp], pl.ds(off, tk)]
                m_new = jnp.maximum(ms[mp], jnp.max(s, axis=0, keepdims=True))
                alpha = jnp.exp2((ms[mp] - m_new) * c)
                p = jnp.exp2((s - m_new) * c).astype(BF16)
                acc_refs[mp][...] = alpha * acc_refs[mp][...] + jnp.dot(vv, p, preferred_element_type=F32)
            else:
                e = jnp.exp2(s * c - bounds[mp])
                m_new = ms[mp] + jnp.sum(e, axis=0, keepdims=True)
                vd = vt_ref[0, 0, data_rows[mp], pl.ds(off, tk)]
                acc_refs[mp][0:vd.shape[0]] += jnp.dot(vd, e.astype(BF16), preferred_element_type=F32)
            out.append(m_new)
        return tuple(out)

    def pair(i, ms):
        j = 2 * i
        scores(j + 1, sb_ref)
        ms = process(j, sa_ref, ms)
        scores(j + 2, sa_ref)
        return process(j + 1, sb_ref, ms)

    m0 = jnp.full((1, tq), NEG_INIT, F32) if bounds is None else jnp.zeros((1, tq), F32)
    ms = (m0, m0)
    scores(0, sa_ref)
    if nk % 2 == 1:
        ms = lax.fori_loop(0, (nk - 1) // 2, pair, ms)
        return process(nk - 1, sa_ref, ms)
    else:
        ms = lax.fori_loop(0, nk // 2 - 1, pair, ms)
        scores(nk - 1, sb_ref)
        ms = process(nk - 2, sa_ref, ms)
        return process(nk - 1, sb_ref, ms)


def _da_attn_kernel(lam_ref, gain_ref, qt_ref, k_ref, vt_ref, *rest, nk, tk, c, out_scale, bounded=False):
    o_ref, acc1_ref, acc2_ref, sa_ref, sb_ref = rest[-5:]
    bounds = (rest[0][0, 0, 0:1, :], rest[0][0, 0, 1:2, :]) if bounded else None
    qt = qt_ref[0]
    sub = lax.broadcasted_iota(jnp.int32, (LANES, 1), 0)
    zero = jnp.zeros_like(qt)
    q1 = jnp.where(sub < DA_HEAD_DIM, qt, zero)
    q2 = jnp.where(sub >= DA_HEAD_DIM, qt, zero)
    full = slice(None)
    dv = LANES
    data = slice(0, dv)
    carry = _attn_t_pipeline((q1, q2), (full, full), (full, full), k_ref, vt_ref, (acc1_ref, acc2_ref), sa_ref,
                             sb_ref, nk=nk, tk=tk, c=c, bounds=bounds, data_rows=(data, data))
    a1 = acc1_ref[...]
    a2 = acc2_ref[...]
    l1, l2 = carry if bounded else (a1[dv:dv + 1], a2[dv:dv + 1])
    o = a1[:dv] / l1 - lam_ref[0] * (a2[:dv] / l2)
    o = o.T
    y = o * lax.rsqrt(jnp.mean(o * o, axis=-1, keepdims=True) + EPS)
    o_ref[0] = (y * gain_ref[...] * out_scale).astype(o_ref.dtype)


def _mla_attn_kernel(qt_ref, k_ref, vt_ref, *rest, nk, tk, c, bounded=False):
    o_ref, acc1_ref, acc2_ref, sa_ref, sb_ref = rest[-5:]
    bounds = (rest[0][0, 0, 0:1, :], rest[0][0, 0, 1:2, :]) if bounded else None
    qa = qt_ref[0, :LANES, :]
    qb = qt_ref[0, LANES:, :]
    vr = MLA_V + ONES_ROWS
    carry = _attn_t_pipeline((qa, qb), (slice(0, LANES), slice(LANES, 2 * LANES)), (slice(0, vr), slice(vr, 2 * vr)),
                             k_ref, vt_ref, (acc1_ref, acc2_ref), sa_ref, sb_ref, nk=nk, tk=tk, c=c, bounds=bounds,
                             data_rows=(slice(0, MLA_V), slice(vr, vr + MLA_V)))
    a1 = acc1_ref[...]
    a2 = acc2_ref[...]
    l1, l2 = carry if bounded else (a1[MLA_V:MLA_V + 1], a2[MLA_V:MLA_V + 1])
    o = jnp.concatenate([a1[:MLA_V] / l1, a2[:MLA_V] / l2], axis=0)
    o_ref[0] = o.T.astype(o_ref.dtype)


def _pick_tk(t):
    return 1408 if t % 1408 == 0 else 256


def _attn_call(kernel, qt, k, vt, extra, *, acc_rows, q0, nq, tq, k0, nk, tk, name, mb=None):
    b, _, t = qt.shape
    g = vt.shape[1]
    qr = qt.shape[1] // g
    kl = k.shape[2] // g
    vr = vt.shape[2]
    body = functools.partial(kernel, nk=nk, tk=tk, bounded=mb is not None)
    mb_specs = [] if mb is None else [pl.BlockSpec((1, 1, 8, tq), lambda bi, gi, qi: (bi, gi, 0, q0 + qi))]
    mb_args = [] if mb is None else [mb]
    extra_specs = [pl.BlockSpec(memory_space=pltpu.SMEM) if e.ndim == 1 else
                   pl.BlockSpec(e.shape, lambda bi, gi, qi: (0, 0)) for e in extra]
    return pl.pallas_call(
        body,
        grid=(b, g, nq),
        in_specs=extra_specs + [
            pl.BlockSpec((1, qr, tq), lambda bi, gi, qi: (bi, gi, q0 + qi)),
            pl.BlockSpec((1, nk * tk, kl), lambda bi, gi, qi: (bi, k0, gi)),
            pl.BlockSpec((1, 1, vr, nk * tk), lambda bi, gi, qi: (bi, gi, 0, k0)),
        ] + mb_specs,
        out_specs=pl.BlockSpec((1, tq, LANES), lambda bi, gi, qi: (bi, qi, gi)),
        out_shape=jax.ShapeDtypeStruct((b, nq * tq, g * LANES), BF16),
        scratch_shapes=[pltpu.VMEM((acc_rows, tq), F32), pltpu.VMEM((acc_rows, tq), F32),
                        pltpu.VMEM((2, tk, tq), F32), pltpu.VMEM((2, tk, tq), F32)],
        compiler_params=pltpu.CompilerParams(
            dimension_semantics=("parallel", "parallel", "arbitrary"), vmem_limit_bytes=VMEM_LIMIT),
        name=name,
    )(*extra, qt, k, vt, *mb_args)


def _attention(kernel, q, k, v, extra, n_lat, dv, name, c):
    b, t, _ = q.shape
    n_ctx = t - n_lat
    qt = jnp.swapaxes(q, 1, 2)
    h = v.shape[2] // dv
    vt = jnp.swapaxes(v, 1, 2).reshape(b, h, dv, t)
    vt = jnp.concatenate([vt, jnp.ones((b, h, ONES_ROWS, t), vt.dtype)], axis=2)
    g = v.shape[2] // LANES
    vt = vt.reshape(b, g, -1, t)
    tk = _pick_tk(t)
    tq = min(ATTN_TQ, n_lat)
    acc_rows = dv + ONES_ROWS
    lat = dict(acc_rows=acc_rows, q0=0, nq=n_lat // tq, tq=tq, k0=0, nk=t // tk, tk=tk, name=name + "_lat")
    qf = q.astype(F32).reshape(b, t, g, 2, -1)
    kf = k.astype(F32).reshape(b, t, g, 2, -1)
    qn = jnp.sqrt(jnp.sum(qf * qf, axis=-1))
    kmax = jnp.max(jnp.sqrt(jnp.sum(kf * kf, axis=-1)), axis=1, keepdims=True)
    mb = (c * qn * kmax).transpose(0, 2, 3, 1)
    mb = jnp.pad(mb, ((0, 0), (0, 0), (0, 6), (0, 0)))
    o_lat = lax.cond(jnp.max(mb) < BOUND_SAFE,
                     lambda: _attn_call(kernel, qt, k, vt, extra, mb=mb, **lat),
                     lambda: _attn_call(kernel, qt, k, vt, extra, **lat))
    o_ctx = _attn_call(kernel, qt, k, vt, extra, acc_rows=acc_rows, q0=n_lat // n_ctx, nq=1, tq=n_ctx,
                       k0=n_lat // n_ctx, nk=1, tk=n_ctx, name=name + "_ctx")
    return jnp.concatenate([o_lat, o_ctx], axis=1)


def _rms_norm(x, g):
    xf = x.astype(F32)
    y = xf * lax.rsqrt(jnp.mean(xf * xf, axis=-1, keepdims=True) + EPS)
    return y * g.astype(F32)


def _l2_norm(x):
    return x * lax.rsqrt(jnp.sum(x * x, axis=-1, keepdims=True) + EPS)


def _rope_tables(rows, dim):
    nf = dim // 4
    inv = ROPE_BASE ** (-jnp.arange(nf, dtype=F32) / nf)
    r = jnp.repeat(jnp.arange(rows, dtype=F32), GRID_W)
    c = jnp.tile(jnp.arange(GRID_W, dtype=F32), rows)
    ang = jnp.concatenate([r[:, None] * inv, c[:, None] * inv], axis=-1)
    return jnp.cos(ang), jnp.sin(ang)


def _pad_rope(cos, sin, n_ctx):
    return (jnp.concatenate([cos, jnp.ones((n_ctx, cos.shape[1]), cos.dtype)], axis=0),
            jnp.concatenate([sin, jnp.zeros((n_ctx, sin.shape[1]), sin.dtype)], axis=0))


def _apply_rope(x, cos, sin):
    half = x.shape[-1] // 2
    x1, x2 = x[..., :half], x[..., half:]
    cs, sn = cos[:, None, :], sin[:, None, :]
    return jnp.concatenate([x1 * cs - x2 * sn, x2 * cs + x1 * sn], axis=-1)


def _diff_attention(hc, hl, w_in, q_gain, k_gain, lam_vecs, sub_gain, w_out, lam_init, cos, sin):
    lv = lam_vecs.astype(F32)
    lam = jnp.exp(jnp.sum(lv[0] * lv[1])) - jnp.exp(jnp.sum(lv[2] * lv[3])) + lam_init
    c = DA_HEAD_DIM ** -0.5 * LOG2E

    n = hl.shape[1]
    h = jnp.concatenate([hl, hc], axis=1)
    b, t, _ = h.shape
    cos, sin = _pad_rope(cos, sin, t - n)
    q, k, v = jnp.split(h @ w_in, 3, axis=-1)
    q = _apply_rope(_rms_norm(q.reshape(b, t, 2 * DA_HEADS, DA_HEAD_DIM), q_gain), cos, sin)
    k = _apply_rope(_rms_norm(k.reshape(b, t, 2 * DA_HEADS, DA_HEAD_DIM), k_gain), cos, sin)
    q = q.reshape(b, t, -1).astype(BF16)
    k = k.reshape(b, t, -1).astype(BF16)
    v = v.astype(BF16)
    kern = functools.partial(_da_attn_kernel, c=c, out_scale=1.0 - lam_init)
    o = _attention(kern, q, k, v, [lam.reshape(1), sub_gain.reshape(1, -1).astype(F32)], n, 2 * DA_HEAD_DIM, "da", c)
    out = o.astype(F32) @ w_out
    return out[:, n:], out[:, :n]


def _mla_attention(hc, hl, w_down, q_a_gain, kv_a_gain, w_uq, w_ukv, q_gain, k_gain, w_out, cos, sin):
    c = MLA_QK ** -0.5 * LOG2E
    pad = LANES - MLA_QK

    n = hl.shape[1]
    h = jnp.concatenate([hl, hc], axis=1)
    cos, sin = _pad_rope(cos, sin, h.shape[1] - n)

    def project(h):
        b, n, _ = h.shape
        lat = h @ w_down
        cq = lat[..., :MLA_Q_RANK]
        ckv = lat[..., MLA_Q_RANK:MLA_Q_RANK + MLA_KV_RANK]
        kr = lat[..., MLA_Q_RANK + MLA_KV_RANK:]
        q = (_rms_norm(cq, q_a_gain) @ w_uq).reshape(b, n, MLA_HEADS, MLA_QK)
        kv = (_rms_norm(ckv, kv_a_gain) @ w_ukv).reshape(b, n, MLA_HEADS, MLA_NOPE + MLA_V)
        k = jnp.concatenate([kv[..., :MLA_NOPE],
                             jnp.broadcast_to(kr[:, :, None, :], (b, n, MLA_HEADS, MLA_ROPE))], axis=-1)
        v = kv[..., MLA_NOPE:]
        q = _rms_norm(q, q_gain)
        k = _rms_norm(k, k_gain)
        q = jnp.concatenate([q[..., :MLA_NOPE], _apply_rope(q[..., MLA_NOPE:], cos, sin)], axis=-1)
        k = jnp.concatenate([k[..., :MLA_NOPE], _apply_rope(k[..., MLA_NOPE:], cos, sin)], axis=-1)
        q = jnp.pad(q, ((0, 0), (0, 0), (0, 0), (0, pad)))
        k = jnp.pad(k, ((0, 0), (0, 0), (0, 0), (0, pad)))
        return q.reshape(b, n, -1), k.reshape(b, n, -1), v.reshape(b, n, -1)

    q, k, v = [a.astype(BF16) for a in project(h)]
    o = _attention(functools.partial(_mla_attn_kernel, c=c), q, k, v, [], n, MLA_V, "mla", c)
    out = o.astype(F32) @ w_out
    return out[:, n:], out[:, :n]


def _centred_depthwise_conv(x, w):
    k = w.shape[0]
    return lax.conv_general_dilated(x, w[:, None, :].astype(x.dtype), window_strides=(1,),
                                    padding=[((k - 1) // 2, k // 2)],
                                    dimension_numbers=('NWC', 'WIO', 'NWC'),
                                    feature_group_count=x.shape[-1])


def _gated_delta_chunked(q, k, v, g, beta, s0):
    b, n, h = q.shape[:3]
    c = GDN_CHUNK

    def to_chunks(t):
        t = t.astype(F32).reshape((b, n // c, c, h) + t.shape[3:])
        return t.transpose((1, 0, 3, 2) + tuple(range(4, t.ndim)))

    q, k, v, g, beta = to_chunks(q), to_chunks(k), to_chunks(v), to_chunks(g), to_chunks(beta)
    decay = jnp.cumsum(g, axis=-1)
    tril = jnp.tril(jnp.ones((c, c), bool))
    strict = jnp.tril(jnp.ones((c, c), bool), -1)
    diff = decay[..., :, None] - decay[..., None, :]
    gamma = jnp.where(tril, jnp.exp(jnp.where(tril, diff, 0.0)), 0.0)
    kb = k * beta[..., None]
    vb = v * beta[..., None]
    a_mat = jnp.where(strict, jnp.einsum('...id,...jd->...ij', kb, k) * gamma, 0.0)
    eye = jnp.eye(c, dtype=F32)
    rhs = jnp.concatenate([vb, kb * jnp.exp(decay)[..., None]], axis=-1)
    sol = lax.linalg.triangular_solve(a_mat + eye, rhs, left_side=True, lower=True, unit_diagonal=True)
    dv = v.shape[-1]
    u, w = sol[..., :dv], sol[..., dv:]
    qk = jnp.where(tril, jnp.einsum('...id,...jd->...ij', q, k) * gamma, 0.0)
    q_dec = q * jnp.exp(decay)[..., None]
    k_dec = k * jnp.exp(decay[..., -1:] - decay)[..., None]
    last = jnp.exp(decay[..., -1])

    def step(s, inp):
        qd, kd, ww, uu, qkc, lst = inp
        v_new = uu - jnp.einsum('bhck,bhkv->bhcv', ww, s)
        o = jnp.einsum('bhck,bhkv->bhcv', qd, s) + jnp.einsum('bhcj,bhjv->bhcv', qkc, v_new)
        s = s * lst[..., None, None] + jnp.einsum('bhck,bhcv->bhkv', kd, v_new)
        return s, o

    s_fin, o = lax.scan(step, s0.astype(F32), (q_dec, k_dec, w, u, qk, last))
    o = o.transpose(1, 0, 3, 2, 4).reshape(b, n, h, dv)
    return o, s_fin


def _gated_deltanet(hc, hl, w_in, conv_w, a_log, dt_bias, o_gain, w_out, need_ctx):
    nqk = GDN_HEADS * GDN_DK
    nv = GDN_HEADS * GDN_DV
    nqkv = 2 * nqk + nv

    def prep(h):
        b, n, _ = h.shape
        p = h @ w_in
        qkv = jax.nn.silu(_centred_depthwise_conv(p[..., :nqkv], conv_w))
        q = _l2_norm(qkv[..., :nqk].reshape(b, n, GDN_HEADS, GDN_DK)) * (GDN_DK ** -0.5)
        k = _l2_norm(qkv[..., nqk:2 * nqk].reshape(b, n, GDN_HEADS, GDN_DK))
        v = qkv[..., 2 * nqk:].reshape(b, n, GDN_HEADS, GDN_DV)
        z = p[..., nqkv:nqkv + nv].reshape(b, n, GDN_HEADS, GDN_DV)
        ab = p[..., nqkv + nv:].astype(F32).reshape(b, n, 2, 2, GDN_HEADS)
        g = -jnp.exp(a_log.astype(F32)) * jax.nn.softplus(ab[:, :, 0] + dt_bias.astype(F32))
        beta = jax.nn.sigmoid(ab[:, :, 1])
        return (q, k, v, g, beta), z

    def scan_dir(t, d, s0):
        q, k, v, g, beta = t
        g, beta = g[:, :, d], beta[:, :, d]
        if d == 1:
            q, k, v, g, beta = [jnp.flip(a, axis=1) for a in (q, k, v, g, beta)]
        o, s = _gated_delta_chunked(q, k, v, g, beta, s0)
        if d == 1:
            o = jnp.flip(o, axis=1)
        return o, s

    tc, zc = prep(hc)
    tl, zl = prep(hl)
    s0 = jnp.zeros((hl.shape[0], GDN_HEADS, GDN_DK, GDN_DV), F32)
    oc_f, sc_f = scan_dir(tc, 0, s0)
    ol_f, _ = scan_dir(tl, 0, sc_f)
    oc_b, sc_b = scan_dir(tc, 1, s0)
    ol_b, _ = scan_dir(tl, 1, sc_b)

    def finish(o, z):
        b, n = o.shape[:2]
        o = _rms_norm(o, o_gain) * jax.nn.silu(z)
        return o.reshape(b, n, -1) @ w_out

    out_l = finish(ol_f + ol_b, zl)
    out_c = finish(oc_f + oc_b, zc) if need_ctx else None
    return out_c, out_l


def _expert_choice_ffn(h, w_router, w_gate, w_up, w_down):
    b, n, _ = h.shape
    cap = EC_FACTOR * n // N_EXPERTS
    aff = jax.nn.softmax((h @ w_router).astype(F32), axis=-1)
    gate, idx = lax.top_k(aff.transpose(0, 2, 1), cap)
    bi = jnp.arange(b)[:, None, None]
    xs = h[bi, idx]
    hg = jnp.einsum('becd,edf->becf', xs, w_gate)
    hu = jnp.einsum('becd,edf->becf', xs, w_up)
    y = jnp.einsum('becf,efd->becd', jax.nn.silu(hg) * hu, w_down) * gate[..., None].astype(h.dtype)
    return jnp.zeros_like(h).at[bi, idx].add(y)


def kernel(x, c, ctx, c_ctx, ada_w, ada_b, norm_g, da_w_in, da_q_gain, da_k_gain, da_lambda, da_sub_gain, da_w_out, mla_w_down, mla_q_a_gain, mla_kv_a_gain, mla_w_uq, mla_w_ukv, mla_q_gain, mla_k_gain, mla_w_out, gdn_w_in, gdn_conv_w, gdn_a_log, gdn_dt_bias, gdn_o_gain, gdn_w_out, moe_router, moe_w_gate, moe_w_up, moe_w_down):
    depth = ada_w.shape[0]
    rows = x.shape[1] // GRID_W
    da_cos, da_sin = _rope_tables(rows, DA_HEAD_DIM)
    mla_cos, mla_sin = _rope_tables(rows, MLA_ROPE)
    cs = ctx
    sc = jax.nn.silu(c)
    scc = jax.nn.silu(c_ctx)
    for l in range(depth):
        last = l == depth - 1
        j = l // N_MIXERS
        kind = l % N_MIXERS
        mod_l = (sc @ ada_w[l] + ada_b[l])[:, None, :]
        mod_c = scc @ ada_w[l] + ada_b[l]
        sh1l, sc1l, g1l, sh2l, sc2l, g2l = jnp.split(mod_l, 6, axis=-1)
        sh1c, sc1c, g1c, sh2c, sc2c, g2c = jnp.split(mod_c, 6, axis=-1)
        hl = _rms_norm(x, norm_g[l, 0]) * (1.0 + sc1l) + sh1l
        hc = _rms_norm(cs, norm_g[l, 0]) * (1.0 + sc1c) + sh1c
        if kind == 0:
            lam_init = 0.8 - 0.6 * math.exp(-0.3 * l)
            mc, ml = _diff_attention(hc, hl, da_w_in[j], da_q_gain[j], da_k_gain[j], da_lambda[j],
                                     da_sub_gain[j], da_w_out[j], lam_init, da_cos, da_sin)
        elif kind == 1:
            mc, ml = _mla_attention(hc, hl, mla_w_down[j], mla_q_a_gain[j], mla_kv_a_gain[j], mla_w_uq[j],
                                    mla_w_ukv[j], mla_q_gain[j], mla_k_gain[j], mla_w_out[j], mla_cos, mla_sin)
        else:
            mc, ml = _gated_deltanet(hc, hl, gdn_w_in[j], gdn_conv_w[j], gdn_a_log[j], gdn_dt_bias[j],
                                     gdn_o_gain[j], gdn_w_out[j], not last)
        x = x + g1l * ml
        hl2 = _rms_norm(x, norm_g[l, 1]) * (1.0 + sc2l) + sh2l
        x = x + g2l * _expert_choice_ffn(hl2, moe_router[l], moe_w_gate[l], moe_w_up[l], moe_w_down[l])
        if not last:
            cs = cs + g1c * mc
            hc2 = _rms_norm(cs, norm_g[l, 1]) * (1.0 + sc2c) + sh2c
            cs = cs + g2c * _expert_choice_ffn(hc2, moe_router[l], moe_w_gate[l], moe_w_up[l], moe_w_down[l])
    return x
```
